```python
import math
import jax, jax.numpy as jnp
from jax import lax
import numpy as np

D_MODEL = 2048
BATCH = 16
SEQ = 256
DEPTH = 4
DEC_BATCH = 4
DEC_SEQ = 4096
PAST_LEN = 512

GRID_W = 64
F_GROUPS = 4
F_GROUP_DIM = 256
F_DIM = F_GROUPS * F_GROUP_DIM
N_HEADS = 8
QK_DIM = 64
V_DIM = 2 * QK_DIM
QK_W = N_HEADS * 2 * QK_DIM
V_W = N_HEADS * V_DIM
SPLIT_SIZES = (F_DIM, F_DIM, QK_W, QK_W, V_W, V_W, D_MODEL, D_MODEL)
IN_COLS = sum(SPLIT_SIZES)
Q_BLOCK = 128
ROPE_BASE = 10000.0
EPS = 1e-6

kernel_name = "hybrid_fourier_diffattn_dit_step"


def rms_norm(x, g):
    xf = x.astype(jnp.float32)
    y = xf * lax.rsqrt(jnp.mean(xf * xf, axis=-1, keepdims=True) + EPS)
    return (y * g.astype(jnp.float32)).astype(x.dtype)


def axial_rope_tables(n_tokens):
    rows = n_tokens // GRID_W
    r = jnp.repeat(jnp.arange(rows, dtype=jnp.float32), GRID_W)
    cidx = jnp.tile(jnp.arange(GRID_W, dtype=jnp.float32), rows)
    half = QK_DIM // 2
    inv = 1.0 / (ROPE_BASE ** (jnp.arange(half // 2, dtype=jnp.float32) * 2.0 / half))
    ang_r = r[:, None] * inv[None, :]
    ang_c = cidx[:, None] * inv[None, :]
    return (jnp.cos(ang_r), jnp.sin(ang_r), jnp.cos(ang_c), jnp.sin(ang_c))


def _rot_half(x, cos, sin):
    x1, x2 = jnp.split(x, 2, axis=-1)
    cos = cos[None, :, None, None, :]
    sin = sin[None, :, None, None, :]
    return jnp.concatenate([x1 * cos - x2 * sin, x2 * cos + x1 * sin], axis=-1)


def apply_axial_rope(x, rope):
    cos_r, sin_r, cos_c, sin_c = rope
    xf = x.astype(jnp.float32)
    xr, xc = jnp.split(xf, 2, axis=-1)
    out = jnp.concatenate([_rot_half(xr, cos_r, sin_r), _rot_half(xc, cos_c, sin_c)], axis=-1)
    return out.astype(x.dtype)


def diff_attention(q, k, v, lam, g_sub, lam_init):
    b, nq = q.shape[0], q.shape[1]
    nblk = nq // Q_BLOCK
    scale = QK_DIM ** -0.5
    kf = k.astype(jnp.float32)
    vf = v.astype(jnp.float32)
    qb = q.reshape(b, nblk, Q_BLOCK, N_HEADS, 2, QK_DIM).transpose(1, 0, 2, 3, 4, 5)

    def one_block(qblk):
        s = jnp.einsum('bqhce,bkhce->bhcqk', qblk.astype(jnp.float32), kf) * scale
        p = jax.nn.softmax(s, axis=-1)
        a = p[:, :, 0] - lam * p[:, :, 1]
        return jnp.einsum('bhqk,bkhd->bqhd', a, vf)

    o = lax.map(one_block, qb)
    o = o.transpose(1, 0, 2, 3, 4).reshape(b, nq, N_HEADS, V_DIM)
    o = o * lax.rsqrt(jnp.mean(o * o, axis=-1, keepdims=True) + EPS)
    o = o * g_sub.astype(jnp.float32) * (1.0 - lam_init)
    return o.astype(q.dtype)


def layer(x, cvec, params, lam_init, rope, k_ctx, v_ctx):
    (w_in, w_fproj, w_aproj, w_out, w_mod, b_mod, g_norm, g_q, g_k, g_sub,
     lq1, lk1, lq2, lk2) = params
    b, n = x.shape[0], x.shape[1]
    mod = jax.nn.silu(cvec) @ w_mod + b_mod
    shift, scale, gate = jnp.split(mod, 3, axis=-1)
    h = rms_norm(x, g_norm) * (1.0 + scale) + shift
    proj = h @ w_in
    idx = [int(i) for i in np.cumsum(SPLIT_SIZES)[:-1]]
    uf, zf, q, k, v, za, gf, ga = jnp.split(proj, idx, axis=-1)

    uf = uf.reshape(b, n, F_GROUPS, F_GROUP_DIM).astype(jnp.float32)
    yf = jnp.fft.fft2(uf, axes=(1, 3), norm='ortho').real.reshape(b, n, F_DIM).astype(x.dtype)
    yf = yf * jax.nn.silu(zf)

    q = rms_norm(q.reshape(b, n, N_HEADS, 2, QK_DIM), g_q)
    k = rms_norm(k.reshape(b, n, N_HEADS, 2, QK_DIM), g_k)
    v = v.reshape(b, n, N_HEADS, V_DIM)
    if rope is not None:
        q = apply_axial_rope(q, rope)
        k = apply_axial_rope(k, rope)
    if k_ctx is None:
        k_all, v_all = k, v
    else:
        k_all = jnp.concatenate([k_ctx.astype(k.dtype), k], axis=1)
        v_all = jnp.concatenate([v_ctx.astype(v.dtype), v], axis=1)
    lam = (jnp.exp(jnp.sum(lq1.astype(jnp.float32) * lk1.astype(jnp.float32)))
           - jnp.exp(jnp.sum(lq2.astype(jnp.float32) * lk2.astype(jnp.float32))) + lam_init)
    ya = diff_attention(q, k_all, v_all, lam, g_sub, lam_init).reshape(b, n, V_W)
    ya = ya * jax.nn.silu(za)

    merged = jax.nn.sigmoid(gf) * (yf @ w_fproj) + jax.nn.sigmoid(ga) * (ya @ w_aproj)
    out = merged @ w_out
    return x + gate * out, k, v


def setup_inputs(seed: int = 0) -> dict:
    key = jax.random.key(seed)
    ks = jax.random.split(key, 20)
    nrm = jax.random.normal
    d = D_MODEL
    return {
        "x_prompt": nrm(ks[0], (BATCH, SEQ, d), jnp.float32),
        "x_sample": nrm(ks[1], (DEC_BATCH, DEC_SEQ, d), jnp.float32),
        "c": nrm(ks[2], (DEC_BATCH, d), jnp.float32),
        "cache_k": nrm(ks[3], (DEC_BATCH, DEPTH, PAST_LEN, N_HEADS, 2, QK_DIM), jnp.float32),
        "cache_v": nrm(ks[4], (DEC_BATCH, DEPTH, PAST_LEN, N_HEADS, V_DIM), jnp.float32),
        "c_ctx": nrm(ks[5], (d,), jnp.float32),
        "w_in": nrm(ks[6], (DEPTH, d, IN_COLS), jnp.float32) * d ** -0.5,
        "w_fproj": nrm(ks[7], (DEPTH, F_DIM, d), jnp.float32) * F_DIM ** -0.5,
        "w_aproj": nrm(ks[8], (DEPTH, V_W, d), jnp.float32) * V_W ** -0.5,
        "w_out": nrm(ks[9], (DEPTH, d, d), jnp.float32) * d ** -0.5,
        "w_mod": nrm(ks[10], (DEPTH, d, 3 * d), jnp.float32) * d ** -0.5,
        "b_mod": nrm(ks[11], (DEPTH, 3 * d), jnp.float32) * 0.01,
        "g_norm": 1.0 + 0.05 * nrm(ks[12], (DEPTH, d), jnp.float32),
        "g_q": 1.0 + 0.05 * nrm(ks[13], (DEPTH, QK_DIM), jnp.float32),
        "g_k": 1.0 + 0.05 * nrm(ks[14], (DEPTH, QK_DIM), jnp.float32),
        "g_sub": 1.0 + 0.05 * nrm(ks[15], (DEPTH, V_DIM), jnp.float32),
        "lam_q1": 0.1 * nrm(ks[16], (DEPTH, QK_DIM), jnp.float32),
        "lam_k1": 0.1 * nrm(ks[17], (DEPTH, QK_DIM), jnp.float32),
        "lam_q2": 0.1 * nrm(ks[18], (DEPTH, QK_DIM), jnp.float32),
        "lam_k2": 0.1 * nrm(ks[19], (DEPTH, QK_DIM), jnp.float32),
    }


def reference(x_prompt, x_sample, c, cache_k, cache_v, c_ctx, w_in, w_fproj, w_aproj, w_out,
              w_mod, b_mod, g_norm, g_q, g_k, g_sub, lam_q1, lam_k1, lam_q2, lam_k2):
    rope = axial_rope_tables(x_sample.shape[1])
    cvec_ctx = c_ctx[None, None, :]
    cvec_lat = c[:, None, :]
    y_p = x_prompt
    y_s = x_sample
    ks_out = []
    vs_out = []
    for l in range(DEPTH):
        params = (w_in[l], w_fproj[l], w_aproj[l], w_out[l], w_mod[l], b_mod[l], g_norm[l],
                  g_q[l], g_k[l], g_sub[l], lam_q1[l], lam_k1[l], lam_q2[l], lam_k2[l])
        lam_init = 0.8 - 0.6 * math.exp(-0.3 * l)
        y_p, k_l, v_l = layer(y_p, cvec_ctx, params, lam_init, None, None, None)
        ks_out.append(k_l)
        vs_out.append(v_l)
        y_s, _, _ = layer(y_s, cvec_lat, params, lam_init, rope, cache_k[:, l], cache_v[:, l])
    new_cache_k = jnp.stack(ks_out, axis=1)
    new_cache_v = jnp.stack(vs_out, axis=1)
    return (y_p, y_s, new_cache_k, new_cache_v)
```

```python
import functools
import math

import numpy as np
import jax
import jax.numpy as jnp
from jax import lax
from jax.experimental import pallas as pl
from jax.experimental.pallas import tpu as pltpu

F32 = jnp.float32
BF16 = jnp.bfloat16

GRID_W = 64
F_GROUPS = 4
F_GROUP_DIM = 256
F_DIM = F_GROUPS * F_GROUP_DIM
N_HEADS = 8
QK_DIM = 64
V_DIM = 2 * QK_DIM
QK_W = N_HEADS * 2 * QK_DIM
V_W = N_HEADS * V_DIM
HEAD_W = 128
LANES = 128
ROPE_BASE = 10000.0
EPS = 1e-6

COL_UF, COL_ZF, COL_Q, COL_K, COL_V, COL_ZA, COL_GF = 0, 1024, 2048, 3072, 4096, 5120, 6144

MOD_ROWS = 8
VMEM_LIMIT = 56 * 1024 * 1024


def _params(sem, vmem=VMEM_LIMIT):
    return pltpu.CompilerParams(dimension_semantics=sem, vmem_limit_bytes=vmem)


def _silu(x):
    return x * jax.nn.sigmoid(x)


def _rope_tables(n_tokens):
    rows = n_tokens // GRID_W
    r = np.repeat(np.arange(rows, dtype=np.float32), GRID_W)
    c = np.tile(np.arange(GRID_W, dtype=np.float32), rows)
    half = QK_DIM // 2
    inv = (1.0 / (np.float32(ROPE_BASE) ** (np.arange(half // 2, dtype=np.float32) * np.float32(2.0) / np.float32(half)))).astype(np.float32)
    ang_r = (r[:, None] * inv[None, :]).astype(np.float32).astype(np.float64)
    ang_c = (c[:, None] * inv[None, :]).astype(np.float32).astype(np.float64)
    cos = np.zeros((n_tokens, QK_DIM)); sa = np.zeros((n_tokens, QK_DIM)); sb = np.zeros((n_tokens, QK_DIM))
    q = half // 2
    for sec, ang in ((0, ang_r), (1, ang_c)):
        base = sec * half
        cos[:, base:base + q] = np.cos(ang); cos[:, base + q:base + half] = np.cos(ang)
        sb[:, base:base + q] = -np.sin(ang)
        sa[:, base + q:base + half] = np.sin(ang)
    tile = lambda t: jnp.asarray(np.tile(t, (1, HEAD_W // QK_DIM)), F32)
    return tile(cos), tile(sa), tile(sb)


def _group_mean_matrix():
    m = np.kron(np.eye(256 // QK_DIM), np.full((QK_DIM, QK_DIM), 1.0 / QK_DIM))
    return jnp.asarray(m, BF16)


def _channel_dft(scale):
    c = np.arange(F_GROUP_DIM)
    ang = 2.0 * np.pi * np.outer(c, c) / F_GROUP_DIM
    return jnp.asarray(np.concatenate([np.cos(ang), np.sin(ang)], axis=0) * scale, BF16)


def _long_dft_tables(n):
    a = np.arange(64)
    f1 = np.zeros((64, 128, 64))
    for b in range(64):
        ang = 2.0 * np.pi * np.outer(a, 64 * a + b) / n
        f1[b, :64] = np.cos(ang)
        f1[b, 64:] = -np.sin(ang)
    ang2 = 2.0 * np.pi * np.outer(a, a) / 64
    c2, s2 = np.cos(ang2), np.sin(ang2)
    w2 = np.block([[c2, s2], [-s2, c2]])
    return jnp.asarray(f1, BF16), jnp.asarray(w2, BF16)


def _short_dft_table(n):
    t = np.arange(n)
    ang = 2.0 * np.pi * np.outer(t, t) / n
    return jnp.asarray(np.concatenate([np.cos(ang), -np.sin(ang)], axis=1), BF16)


def _mod_kernel(c_ref, w_ref, b_ref, o_ref):
    s = _silu(c_ref[...])
    o_ref[...] = jnp.dot(s.astype(BF16), w_ref[...].astype(BF16), preferred_element_type=F32) + b_ref[...]


def _modulation(cvecs, w_mod, b_mod):
    depth, d, d3 = w_mod.shape
    tn = 1024 if d3 % 1024 == 0 else d3
    return pl.pallas_call(
        _mod_kernel,
        grid=(depth, d3 // tn),
        in_specs=[pl.BlockSpec((MOD_ROWS, d), lambda l, j: (0, 0)),
                  pl.BlockSpec((None, d, tn), lambda l, j: (l, 0, j)),
                  pl.BlockSpec((None, 1, tn), lambda l, j: (l, 0, j))],
        out_specs=pl.BlockSpec((None, MOD_ROWS, tn), lambda l, j: (l, 0, j)),
        out_shape=jax.ShapeDtypeStruct((depth, MOD_ROWS, d3), F32),
        compiler_params=_params(("arbitrary", "arbitrary")),
        name="modulation",
    )(cvecs, w_mod, b_mod.reshape(depth, 1, d3))


def _inproj_kernel(x_ref, mod_ref, g_ref, w_ref, o_ref, h_ref, *, d):
    @pl.when(pl.program_id(1) == 0)
    def _():
        x = x_ref[...]
        ms = jnp.mean(x * x, axis=-1, keepdims=True)
        shift = mod_ref[:, 0:d]
        scale = mod_ref[:, d:2 * d]
        h = x * lax.rsqrt(ms + EPS) * (g_ref[...] * (1.0 + scale)) + shift
        h_ref[...] = h.astype(BF16)

    o_ref[...] = jnp.dot(h_ref[...], w_ref[...], preferred_element_type=F32).astype(o_ref.dtype)


def _in_projection(x2d, mod_l, g_norm_l, w_in_l, *, rows_per_batch, mod_row0, per_batch_mod):
    m, d = x2d.shape
    n_cols = w_in_l.shape[1]
    tm = min(1024, rows_per_batch if per_batch_mod else m)
    tn = 1024 if n_cols % 1024 == 0 else 512
    blocks_per_batch = rows_per_batch // tm
    if per_batch_mod:
        mod_map = lambda i, j: (mod_row0 + i // blocks_per_batch, 0, 0)
    else:
        mod_map = lambda i, j: (mod_row0, 0, 0)
    return pl.pallas_call(
        functools.partial(_inproj_kernel, d=d),
        grid=(m // tm, n_cols // tn),
        in_specs=[pl.BlockSpec((tm, d), lambda i, j: (i, 0)),
                  pl.BlockSpec((None, 1, 3 * d), mod_map),
                  pl.BlockSpec((1, d), lambda i, j: (0, 0)),
                  pl.BlockSpec((d, tn), lambda i, j: (0, j))],
        out_specs=pl.BlockSpec((tm, tn), lambda i, j: (i, j)),
        out_shape=jax.ShapeDtypeStruct((m, n_cols), BF16),
        scratch_shapes=[pltpu.VMEM((tm, d), BF16)],
        compiler_params=_params(("arbitrary", "arbitrary")),
        name="in_projection",
    )(x2d, mod_l.reshape(MOD_ROWS, 1, 3 * d), g_norm_l.reshape(1, d), w_in_l)


def _prep_kernel(*refs, use_rope, emit_k_f32):
    q_ref, k_ref, v_ref, gq_ref, gk_ref, gm_ref = refs[:6]
    refs = refs[6:]
    if use_rope:
        cos_ref, sa_ref, sb_ref = refs[:3]
        refs = refs[3:]
    qo_ref, ko_ref, vt_ref = refs[:3]
    kf_ref = refs[3] if emit_k_f32 else None

    def group_norm(x, g):
        sq = (x * x).astype(BF16)
        parts = [jnp.dot(sq[:, i * 256:(i + 1) * 256], gm_ref[...], preferred_element_type=F32)
                 for i in range(QK_W // 256)]
        ms = jnp.concatenate(parts, axis=1)
        return x * lax.rsqrt(ms + EPS) * g

    def rope(x):
        cos, sa, sb = cos_ref[...], sa_ref[...], sb_ref[...]
        outs = []
        for i in range(QK_W // HEAD_W):
            xc = x[:, i * HEAD_W:(i + 1) * HEAD_W]
            outs.append(xc * cos + pltpu.roll(xc, 16, 1) * sa + pltpu.roll(xc, HEAD_W - 16, 1) * sb)
        return jnp.concatenate(outs, axis=1)

    q = group_norm(q_ref[...].astype(F32), gq_ref[...])
    k = group_norm(k_ref[...].astype(F32), gk_ref[...])
    if use_rope:
        q, k = rope(q), rope(k)
    qo_ref[...] = (q * (QK_DIM ** -0.5)).astype(BF16)
    ko_ref[...] = k.astype(BF16)
    if emit_k_f32:
        kf_ref[...] = k
    for h in range(N_HEADS):
        vh = v_ref[:, h * HEAD_W:(h + 1) * HEAD_W].astype(F32)
        vt_ref[h * HEAD_W:(h + 1) * HEAD_W, :] = vh.T.astype(BF16)


def _qkv_prep(proj, g_q_l, g_k_l, rope_tabs, *, emit_k_f32):
    b, n, _ = proj.shape
    tm = min(512, n)
    use_rope = rope_tabs is not None
    reps = QK_W // QK_DIM
    in_specs = [pl.BlockSpec((None, tm, QK_W), lambda bi, i: (bi, i, COL_Q // QK_W)),
                pl.BlockSpec((None, tm, QK_W), lambda bi, i: (bi, i, COL_K // QK_W)),
                pl.BlockSpec((None, tm, V_W), lambda bi, i: (bi, i, COL_V // V_W)),
                pl.BlockSpec((1, QK_W), lambda bi, i: (0, 0)),
                pl.BlockSpec((1, QK_W), lambda bi, i: (0, 0)),
                pl.BlockSpec((256, 256), lambda bi, i: (0, 0))]
    args = [proj, proj, proj, jnp.tile(g_q_l, reps).reshape(1, QK_W), jnp.tile(g_k_l, reps).reshape(1, QK_W),
            _group_mean_matrix()]
    if use_rope:
        in_specs += [pl.BlockSpec((tm, HEAD_W), lambda bi, i: (i, 0))] * 3
        args += list(rope_tabs)
    out_specs = [pl.BlockSpec((None, tm, QK_W), lambda bi, i: (bi, i, 0)),
                 pl.BlockSpec((None, tm, QK_W), lambda bi, i: (bi, i, 0)),
                 pl.BlockSpec((None, V_W, tm), lambda bi, i: (bi, 0, i))]
    out_shape = [jax.ShapeDtypeStruct((b, n, QK_W), BF16),
                 jax.ShapeDtypeStruct((b, n, QK_W), BF16),
                 jax.ShapeDtypeStruct((b, V_W, n), BF16)]
    if emit_k_f32:
        out_specs.append(pl.BlockSpec((None, tm, QK_W), lambda bi, i: (bi, i, 0)))
        out_shape.append(jax.ShapeDtypeStruct((b, n, QK_W), F32))
    return pl.pallas_call(
        functools.partial(_prep_kernel, use_rope=use_rope, emit_k_f32=emit_k_f32),
        grid=(b, n // tm),
        in_specs=in_specs, out_specs=out_specs, out_shape=out_shape,
        compiler_params=_params(("arbitrary", "arbitrary")),
        name="qkv_prep",
    )(*args)


def _attn_kernel(*refs, n_new, n_cache, k_chunk, lam_init):
    q_ref, kn_ref, vnt_ref = refs[:3]
    refs = refs[3:]
    if n_cache:
        kc_ref, vc_ref = refs[:2]
        refs = refs[2:]
    za_ref, gsub_ref, lamp_ref, o_ref = refs[:4]
    if n_cache:
        kcs_ref, vcts_ref = refs[4:6]

        @pl.when(pl.program_id(2) == 0)
        def _():
            kcs_ref[...] = kc_ref[...].astype(BF16)
            vcts_ref[...] = vc_ref[...].T.astype(BF16)

    q = q_ref[...]
    lane = lax.broadcasted_iota(jnp.int32, q.shape, 1)
    zero = jnp.zeros_like(q)
    q_comp = (jnp.where(lane < QK_DIM, q, zero), jnp.where(lane >= QK_DIM, q, zero))

    chunks = []
    if n_cache:
        chunks.append((lambda: kcs_ref[...], lambda: vcts_ref[...]))
    for j in range(n_new // k_chunk):
        chunks.append((lambda j=j: kn_ref[j * k_chunk:(j + 1) * k_chunk, :],
                       lambda j=j: vnt_ref[:, j * k_chunk:(j + 1) * k_chunk]))

    outs = []
    for c in range(2):
        m = l = acc = None
        for k_get, vt_get in chunks:
            s = lax.dot_general(k_get(), q_comp[c], (((1,), (1,)), ((), ())), preferred_element_type=F32)
            mc = jnp.max(s, axis=0, keepdims=True)
            if m is None:
                m = mc
                p = jnp.exp(s - m)
                l = jnp.sum(p, axis=0, keepdims=True)
                acc = jnp.dot(vt_get(), p.astype(BF16), preferred_element_type=F32)
            else:
                m_new = jnp.maximum(m, mc)
                alpha = jnp.exp(m - m_new)
                p = jnp.exp(s - m_new)
                l = alpha * l + jnp.sum(p, axis=0, keepdims=True)
                acc = alpha * acc + jnp.dot(vt_get(), p.astype(BF16), preferred_element_type=F32)
                m = m_new
        outs.append(acc / l)

    lp = lamp_ref[...]
    lam = (jnp.exp(jnp.sum(lp[0:1] * lp[1:2], axis=1, keepdims=True))
           - jnp.exp(jnp.sum(lp[2:3] * lp[3:4], axis=1, keepdims=True)) + lam_init)
    o_t = outs[0] - lam * outs[1]
    ms = jnp.mean(o_t * o_t, axis=0, keepdims=True)
    o = (o_t * lax.rsqrt(ms + EPS)).T
    za = za_ref[...].astype(F32)
    o = o * (gsub_ref[...] * (1.0 - lam_init)) * _silu(za)
    o_ref[...] = o.astype(o_ref.dtype)


def _diff_attention(qn, kn, vnt, proj, g_sub_l, lam_params_l, cache_k4, cache_v4, layer_idx, lam_init):
    b, n, _ = qn.shape
    tq = min(256, n)
    k_chunk = min(512, n)
    n_cache = 0 if cache_k4 is None else cache_k4.shape[2]
    in_specs = [pl.BlockSpec((None, tq, HEAD_W), lambda bi, h, i: (bi, i, h)),
                pl.BlockSpec((None, n, HEAD_W), lambda bi, h, i: (bi, 0, h)),
                pl.BlockSpec((None, HEAD_W, n), lambda bi, h, i: (bi, h, 0))]
    args = [qn, kn, vnt]
    scratch = []
    if n_cache:
        in_specs += [pl.BlockSpec((None, None, n_cache, HEAD_W), lambda bi, h, i: (bi, layer_idx, 0, h)),
                     pl.BlockSpec((None, None, n_cache, HEAD_W), lambda bi, h, i: (bi, layer_idx, 0, h))]
        args += [cache_k4, cache_v4]
        scratch = [pltpu.VMEM((n_cache, HEAD_W), BF16), pltpu.VMEM((HEAD_W, n_cache), BF16)]
    in_specs += [pl.BlockSpec((None, tq, HEAD_W), lambda bi, h, i: (bi, i, COL_ZA // HEAD_W + h)),
                 pl.BlockSpec((1, V_DIM), lambda bi, h, i: (0, 0)),
                 pl.BlockSpec((4, QK_DIM), lambda bi, h, i: (0, 0))]
    args += [proj, g_sub_l.reshape(1, V_DIM), lam_params_l]
    return pl.pallas_call(
        functools.partial(_attn_kernel, n_new=n, n_cache=n_cache, k_chunk=k_chunk, lam_init=lam_init),
        grid=(b, N_HEADS, n // tq),
        in_specs=in_specs,
        out_specs=pl.BlockSpec((None, tq, HEAD_W), lambda bi, h, i: (bi, i, h)),
        out_shape=jax.ShapeDtypeStruct((b, n, V_W), BF16),
        scratch_shapes=scratch,
        compiler_params=_params(("arbitrary", "arbitrary", "arbitrary")),
        name="diff_attention",
    )(*args)


def _fourier_long_kernel(u_ref, zf_ref, f1_ref, w2_ref, cd_ref, o_ref, x_ref, a_ref, z_ref, *, n):
    side = 64
    halves = F_GROUP_DIM // LANES

    def lane_half(v, hf):
        return v[:, hf * LANES:(hf + 1) * LANES]

    def strided_rows(ref, idx, start):
        return ref[idx, pl.ds(start, side, stride=side), :]

    for hf in range(halves):
        x_ref[hf] = u_ref[:, hf * LANES:(hf + 1) * LANES].astype(F32)
    for b in range(side):
        xb = jnp.concatenate([strided_rows(x_ref, hf, b) for hf in range(halves)], axis=1).astype(BF16)
        ab = jnp.dot(f1_ref[b], xb, preferred_element_type=F32)
        for ri in range(2):
            for hf in range(halves):
                a_ref[2 * ri + hf, b * side:(b + 1) * side, :] = lane_half(ab[ri * side:(ri + 1) * side], hf)
    for c in range(side):
        ac = jnp.concatenate(
            [jnp.concatenate([strided_rows(a_ref, 2 * ri + hf, c) for hf in range(halves)], axis=1)
             for ri in range(2)], axis=0).astype(BF16)
        zc = jnp.dot(w2_ref[...], ac, preferred_element_type=F32)
        for ri in range(2):
            for hf in range(halves):
                z_ref[2 * ri + hf, pl.ds(c, side, stride=side), :] = lane_half(zc[ri * side:(ri + 1) * side], hf)
    rows = 512
    cd = cd_ref[...]
    for r in range(n // rows):
        sl = slice(r * rows, (r + 1) * rows)
        zr = jnp.concatenate([z_ref[hf, sl, :] for hf in range(halves)], axis=1).astype(BF16)
        zi = jnp.concatenate([z_ref[2 + hf, sl, :] for hf in range(halves)], axis=1).astype(BF16)
        y = (jnp.dot(zr, cd[:F_GROUP_DIM], preferred_element_type=F32)
             + jnp.dot(zi, cd[F_GROUP_DIM:], preferred_element_type=F32))
        o_ref[sl, :] = (y * _silu(zf_ref[sl, :].astype(F32))).astype(o_ref.dtype)


def _fourier_short_kernel(u_ref, zf_ref, fn_ref, cd_ref, o_ref, *, n):
    cd = cd_ref[...]
    u = u_ref[...]
    tc = jnp.dot(u, cd[:F_GROUP_DIM], preferred_element_type=F32).astype(BF16)
    ts = jnp.dot(u, cd[F_GROUP_DIM:], preferred_element_type=F32).astype(BF16)
    fn = fn_ref[...]
    y = (jnp.dot(fn[:, :n], tc, preferred_element_type=F32)
         + jnp.dot(fn[:, n:], ts, preferred_element_type=F32))
    o_ref[...] = (y * _silu(zf_ref[...].astype(F32))).astype(o_ref.dtype)


def _fourier_branch(proj):
    b, n, _ = proj.shape
    gc = F_GROUP_DIM
    scale = 1.0 / math.sqrt(n * gc)
    cd = _channel_dft(scale)
    u_spec = pl.BlockSpec((None, n, gc), lambda bi, g: (bi, 0, COL_UF // gc + g))
    zf_spec = pl.BlockSpec((None, n, gc), lambda bi, g: (bi, 0, COL_ZF // gc + g))
    cd_spec = pl.BlockSpec((2 * gc, gc), lambda bi, g: (0, 0))
    out_spec = pl.BlockSpec((None, n, gc), lambda bi, g: (bi, 0, g))
    out_shape = jax.ShapeDtypeStruct((b, n, F_DIM), BF16)
    if n == 64 * 64:
        f1, w2 = _long_dft_tables(n)
        return pl.pallas_call(
            functools.partial(_fourier_long_kernel, n=n),
            grid=(b, F_GROUPS),
            in_specs=[u_spec, zf_spec,
                      pl.BlockSpec((64, 128, 64), lambda bi, g: (0, 0, 0)),
                      pl.BlockSpec((128, 128), lambda bi, g: (0, 0)),
                      cd_spec],
            out_specs=out_spec, out_shape=out_shape,
            scratch_shapes=[pltpu.VMEM((gc // LANES, n, LANES), F32),
                            pltpu.VMEM((2 * gc // LANES, n, LANES), F32),
                            pltpu.VMEM((2 * gc // LANES, n, LANES), F32)],
            compiler_params=_params(("arbitrary", "arbitrary")),
            name="fourier_long",
        )(proj, proj, f1, w2, cd)
    fn = _short_dft_table(n)
    return pl.pallas_call(
        functools.partial(_fourier_short_kernel, n=n),
        grid=(b, F_GROUPS),
        in_specs=[u_spec, zf_spec, pl.BlockSpec((n, 2 * n), lambda bi, g: (0, 0)), cd_spec],
        out_specs=out_spec, out_shape=out_shape,
        compiler_params=_params(("arbitrary", "arbitrary")),
        name="fourier_short",
    )(proj, proj, fn, cd)


def _outproj_kernel(yf_ref, ya_ref, gf_ref, ga_ref, x_ref, mod_ref, wf_ref, wa_ref, wo_ref, o_ref, *, d):
    f = jnp.dot(yf_ref[...], wf_ref[...], preferred_element_type=F32)
    a = jnp.dot(ya_ref[...], wa_ref[...], preferred_element_type=F32)
    merged = jax.nn.sigmoid(gf_ref[...].astype(F32)) * f + jax.nn.sigmoid(ga_ref[...].astype(F32)) * a
    out = jnp.dot(merged.astype(BF16), wo_ref[...], preferred_element_type=F32)
    gate = mod_ref[:, 2 * d:3 * d]
    o_ref[...] = x_ref[...] + gate * out


def _out_projection(yf2d, ya2d, proj2d, x2d, mod_l, wf, wa, wo, *, rows_per_batch, mod_row0, per_batch_mod):
    m, d = x2d.shape
    tm = min(256, rows_per_batch)
    blocks_per_batch = rows_per_batch // tm
    if per_batch_mod:
        mod_map = lambda i: (mod_row0 + i // blocks_per_batch, 0, 0)
    else:
        mod_map = lambda i: (mod_row0, 0, 0)
    const = lambda shape: pl.BlockSpec(shape, lambda i: (0, 0), pipeline_mode=pl.Buffered(1))
    return pl.pallas_call(
        functools.partial(_outproj_kernel, d=d),
        grid=(m // tm,),
        in_specs=[pl.BlockSpec((tm, F_DIM), lambda i: (i, 0)),
                  pl.BlockSpec((tm, V_W), lambda i: (i, 0)),
                  pl.BlockSpec((tm, d), lambda i: (i, COL_GF // d)),
                  pl.BlockSpec((tm, d), lambda i: (i, COL_GF // d + 1)),
                  pl.BlockSpec((tm, d), lambda i: (i, 0)),
                  pl.BlockSpec((None, 1, 3 * d), mod_map),
                  const((F_DIM, d)), const((V_W, d)), const((d, d))],
        out_specs=pl.BlockSpec((tm, d), lambda i: (i, 0)),
        out_shape=jax.ShapeDtypeStruct((m, d), F32),
        compiler_params=_params(("arbitrary",)),
        name="out_projection",
    )(yf2d, ya2d, proj2d, proj2d, x2d, mod_l.reshape(MOD_ROWS, 1, 3 * d), wf, wa, wo)


def _layer(x, mod_l, weights_l, lam_init, *, mod_row0, per_batch_mod, rope_tabs, cache_k4, cache_v4, layer_idx,
           emit_cache):
    w_in, w_fproj, w_aproj, w_out, g_norm, g_q, g_k, g_sub, lam_params = weights_l
    b, n, d = x.shape
    x2d = x.reshape(b * n, d)
    proj2d = _in_projection(x2d, mod_l, g_norm, w_in, rows_per_batch=n, mod_row0=mod_row0,
                            per_batch_mod=per_batch_mod)
    proj = proj2d.reshape(b, n, -1)
    prep = _qkv_prep(proj, g_q, g_k, rope_tabs, emit_k_f32=emit_cache)
    qn, kn, vnt = prep[:3]
    ya = _diff_attention(qn, kn, vnt, proj, g_sub, lam_params, cache_k4, cache_v4, layer_idx, lam_init)
    yf = _fourier_branch(proj)
    y2d = _out_projection(yf.reshape(b * n, F_DIM), ya.reshape(b * n, V_W), proj2d, x2d, mod_l,
                          w_fproj, w_aproj, w_out, rows_per_batch=n, mod_row0=mod_row0,
                          per_batch_mod=per_batch_mod)
    y = y2d.reshape(b, n, d)
    if emit_cache:
        k_new = prep[3].reshape(b, n, N_HEADS, 2, QK_DIM)
        v_new = proj[:, :, COL_V:COL_V + V_W].astype(F32).reshape(b, n, N_HEADS, V_DIM)
        return y, k_new, v_new
    return y, None, None


def kernel(x_prompt, x_sample, c, cache_k, cache_v, c_ctx, w_in, w_fproj, w_aproj, w_out, w_mod, b_mod, g_norm,
           g_q, g_k, g_sub, lam_q1, lam_k1, lam_q2, lam_k2):
    depth = w_in.shape[0]
    d = x_prompt.shape[-1]
    dec_b, dec_n = x_sample.shape[0], x_sample.shape[1]
    assert 1 + dec_b <= MOD_ROWS

    cvecs = jnp.concatenate([c_ctx[None, :], c, jnp.zeros((MOD_ROWS - 1 - dec_b, d), F32)], axis=0)
    mod = _modulation(cvecs, w_mod, b_mod)

    w_in_b, w_fproj_b, w_aproj_b, w_out_b = (w.astype(BF16) for w in (w_in, w_fproj, w_aproj, w_out))
    lam_params = jnp.stack([lam_q1, lam_k1, lam_q2, lam_k2], axis=1)
    rope_tabs = _rope_tables(dec_n)
    past = cache_k.shape[2]
    cache_k4 = cache_k.reshape(dec_b, depth, past, QK_W)
    cache_v4 = cache_v.reshape(dec_b, depth, past, V_W)

    y_p, y_s = x_prompt, x_sample
    ks_out, vs_out = [], []
    for l in range(depth):
        lam_init = 0.8 - 0.6 * math.exp(-0.3 * l)
        weights_l = (w_in_b[l], w_fproj_b[l], w_aproj_b[l], w_out_b[l], g_norm[l], g_q[l], g_k[l], g_sub[l],
                     lam_params[l])
        y_p, k_l, v_l = _layer(y_p, mod[l], weights_l, lam_init, mod_row0=0, per_batch_mod=False, rope_tabs=None,
                               cache_k4=None, cache_v4=None, layer_idx=l, emit_cache=True)
        ks_out.append(k_l)
        vs_out.append(v_l)
        y_s, _, _ = _layer(y_s, mod[l], weights_l, lam_init, mod_row0=1, per_batch_mod=True, rope_tabs=rope_tabs,
                           cache_k4=cache_k4, cache_v4=cache_v4, layer_idx=l, emit_cache=False)
    return (y_p, y_s, jnp.stack(ks_out, axis=1), jnp.stack(vs_out, axis=1))
```

```python
import functools
import math

import numpy as np
import jax
import jax.numpy as jnp
from jax import lax
from jax.experimental import pallas as pl
from jax.experimental.pallas import tpu as pltpu

F32 = jnp.float32
BF16 = jnp.bfloat16

GRID_W = 64
F_GROUPS = 4
F_GROUP_DIM = 256
F_DIM = F_GROUPS * F_GROUP_DIM
N_HEADS = 8
QK_DIM = 64
V_DIM = 2 * QK_DIM
QK_W = N_HEADS * 2 * QK_DIM
V_W = N_HEADS * V_DIM
HEAD_W = 128
LANES = 128
ROPE_BASE = 10000.0
EPS = 1e-6

COL_UF, COL_ZF, COL_Q, COL_K, COL_V, COL_ZA, COL_GF = 0, 1024, 2048, 3072, 4096, 5120, 6144

MOD_ROWS = 8
VMEM_LIMIT = 56 * 1024 * 1024


def _params(sem, vmem=VMEM_LIMIT):
    return pltpu.CompilerParams(dimension_semantics=sem, vmem_limit_bytes=vmem)


def _silu(x):
    return x * jax.nn.sigmoid(x)


def _rope_tables(n_tokens):
    rows = n_tokens // GRID_W
    r = np.repeat(np.arange(rows, dtype=np.float32), GRID_W)
    c = np.tile(np.arange(GRID_W, dtype=np.float32), rows)
    half = QK_DIM // 2
    inv = (1.0 / (np.float32(ROPE_BASE) ** (np.arange(half // 2, dtype=np.float32) * np.float32(2.0) / np.float32(half)))).astype(np.float32)
    ang_r = (r[:, None] * inv[None, :]).astype(np.float32).astype(np.float64)
    ang_c = (c[:, None] * inv[None, :]).astype(np.float32).astype(np.float64)
    cos = np.zeros((n_tokens, QK_DIM)); sa = np.zeros((n_tokens, QK_DIM)); sb = np.zeros((n_tokens, QK_DIM))
    q = half // 2
    for sec, ang in ((0, ang_r), (1, ang_c)):
        base = sec * half
        cos[:, base:base + q] = np.cos(ang); cos[:, base + q:base + half] = np.cos(ang)
        sb[:, base:base + q] = -np.sin(ang)
        sa[:, base + q:base + half] = np.sin(ang)
    tile = lambda t: jnp.asarray(np.tile(t, (1, HEAD_W // QK_DIM)), F32)
    return tile(cos), tile(sa), tile(sb)


def _group_mean_matrix():
    m = np.kron(np.eye(256 // QK_DIM), np.full((QK_DIM, QK_DIM), 1.0 / QK_DIM))
    return jnp.asarray(m, BF16)


def _channel_dft(scale):
    c = np.arange(F_GROUP_DIM)
    ang = 2.0 * np.pi * np.outer(c, c) / F_GROUP_DIM
    return jnp.asarray(np.concatenate([np.cos(ang), np.sin(ang)], axis=0) * scale, BF16)


def _long_dft_tables(n):
    a = np.arange(64)
    f1 = np.zeros((64, 128, 64))
    for b in range(64):
        ang = 2.0 * np.pi * np.outer(a, 64 * a + b) / n
        f1[b, :64] = np.cos(ang)
        f1[b, 64:] = -np.sin(ang)
    ang2 = 2.0 * np.pi * np.outer(a, a) / 64
    c2, s2 = np.cos(ang2), np.sin(ang2)
    w2 = np.block([[c2, s2], [-s2, c2]])
    return jnp.asarray(f1, BF16), jnp.asarray(w2, BF16)


def _short_dft_table(n):
    t = np.arange(n)
    ang = 2.0 * np.pi * np.outer(t, t) / n
    return jnp.asarray(np.concatenate([np.cos(ang), -np.sin(ang)], axis=1), BF16)


def _mod_kernel(c_ref, w_ref, b_ref, o_ref):
    s = _silu(c_ref[...])
    o_ref[...] = jnp.dot(s.astype(BF16), w_ref[...].astype(BF16), preferred_element_type=F32) + b_ref[...]


def _modulation(cvecs, w_mod, b_mod):
    depth, d, d3 = w_mod.shape
    tn = 1024 if d3 % 1024 == 0 else d3
    return pl.pallas_call(
        _mod_kernel,
        grid=(depth, d3 // tn),
        in_specs=[pl.BlockSpec((MOD_ROWS, d), lambda l, j: (0, 0)),
                  pl.BlockSpec((None, d, tn), lambda l, j: (l, 0, j)),
                  pl.BlockSpec((None, 1, tn), lambda l, j: (l, 0, j))],
        out_specs=pl.BlockSpec((None, MOD_ROWS, tn), lambda l, j: (l, 0, j)),
        out_shape=jax.ShapeDtypeStruct((depth, MOD_ROWS, d3), F32),
        compiler_params=_params(("arbitrary", "arbitrary")),
        name="modulation",
    )(cvecs, w_mod, b_mod.reshape(depth, 1, d3))


def _inproj_kernel(x_ref, mod_ref, g_ref, w_ref, o_ref, h_ref, *, d):
    @pl.when(pl.program_id(1) == 0)
    def _():
        x = x_ref[...]
        ms = jnp.mean(x * x, axis=-1, keepdims=True)
        shift = mod_ref[:, 0:d]
        scale = mod_ref[:, d:2 * d]
        h = x * lax.rsqrt(ms + EPS) * (g_ref[...] * (1.0 + scale)) + shift
        h_ref[...] = h.astype(BF16)

    o_ref[...] = jnp.dot(h_ref[...], w_ref[...], preferred_element_type=F32).astype(o_ref.dtype)


def _in_projection(x2d, mod_l, g_norm_l, w_in_l, *, rows_per_batch, mod_row0, per_batch_mod):
    m, d = x2d.shape
    n_cols = w_in_l.shape[1]
    tm = min(1024, rows_per_batch if per_batch_mod else m)
    tn = 1024 if n_cols % 1024 == 0 else 512
    blocks_per_batch = rows_per_batch // tm
    if per_batch_mod:
        mod_map = lambda i, j: (mod_row0 + i // blocks_per_batch, 0, 0)
    else:
        mod_map = lambda i, j: (mod_row0, 0, 0)
    return pl.pallas_call(
        functools.partial(_inproj_kernel, d=d),
        grid=(m // tm, n_cols // tn),
        in_specs=[pl.BlockSpec((tm, d), lambda i, j: (i, 0)),
                  pl.BlockSpec((None, 1, 3 * d), mod_map),
                  pl.BlockSpec((1, d), lambda i, j: (0, 0)),
                  pl.BlockSpec((d, tn), lambda i, j: (0, j))],
        out_specs=pl.BlockSpec((tm, tn), lambda i, j: (i, j)),
        out_shape=jax.ShapeDtypeStruct((m, n_cols), BF16),
        scratch_shapes=[pltpu.VMEM((tm, d), BF16)],
        compiler_params=_params(("arbitrary", "arbitrary")),
        name="in_projection",
    )(x2d, mod_l.reshape(MOD_ROWS, 1, 3 * d), g_norm_l.reshape(1, d), w_in_l)


def _prep_kernel(*refs, use_rope, emit_k_f32):
    q_ref, k_ref, v_ref, gq_ref, gk_ref, gm_ref = refs[:6]
    refs = refs[6:]
    if use_rope:
        cos_ref, sa_ref, sb_ref = refs[:3]
        refs = refs[3:]
    qo_ref, ko_ref, vt_ref = refs[:3]
    kf_ref = refs[3] if emit_k_f32 else None

    def group_norm(x, g):
        sq = (x * x).astype(BF16)
        parts = [jnp.dot(sq[:, i * 256:(i + 1) * 256], gm_ref[...], preferred_element_type=F32)
                 for i in range(QK_W // 256)]
        ms = jnp.concatenate(parts, axis=1)
        return x * lax.rsqrt(ms + EPS) * g

    def rope(x):
        cos, sa, sb = cos_ref[...], sa_ref[...], sb_ref[...]
        outs = []
        for i in range(QK_W // HEAD_W):
            xc = x[:, i * HEAD_W:(i + 1) * HEAD_W]
            outs.append(xc * cos + pltpu.roll(xc, 16, 1) * sa + pltpu.roll(xc, HEAD_W - 16, 1) * sb)
        return jnp.concatenate(outs, axis=1)

    q = group_norm(q_ref[...].astype(F32), gq_ref[...])
    k = group_norm(k_ref[...].astype(F32), gk_ref[...])
    if use_rope:
        q, k = rope(q), rope(k)
    qo_ref[...] = (q * (QK_DIM ** -0.5 * math.log2(math.e))).astype(BF16)
    ko_ref[...] = k.astype(BF16)
    if emit_k_f32:
        kf_ref[...] = k
    for h in range(N_HEADS):
        vh = v_ref[:, h * HEAD_W:(h + 1) * HEAD_W].astype(F32)
        vt_ref[h * HEAD_W:(h + 1) * HEAD_W, :] = vh.T.astype(BF16)


def _qkv_prep(proj, g_q_l, g_k_l, rope_tabs, *, emit_k_f32):
    b, n, _ = proj.shape
    tm = min(512, n)
    use_rope = rope_tabs is not None
    reps = QK_W // QK_DIM
    in_specs = [pl.BlockSpec((None, tm, QK_W), lambda bi, i: (bi, i, COL_Q // QK_W)),
                pl.BlockSpec((None, tm, QK_W), lambda bi, i: (bi, i, COL_K // QK_W)),
                pl.BlockSpec((None, tm, V_W), lambda bi, i: (bi, i, COL_V // V_W)),
                pl.BlockSpec((1, QK_W), lambda bi, i: (0, 0)),
                pl.BlockSpec((1, QK_W), lambda bi, i: (0, 0)),
                pl.BlockSpec((256, 256), lambda bi, i: (0, 0))]
    args = [proj, proj, proj, jnp.tile(g_q_l, reps).reshape(1, QK_W), jnp.tile(g_k_l, reps).reshape(1, QK_W),
            _group_mean_matrix()]
    if use_rope:
        in_specs += [pl.BlockSpec((tm, HEAD_W), lambda bi, i: (i, 0))] * 3
        args += list(rope_tabs)
    out_specs = [pl.BlockSpec((None, tm, QK_W), lambda bi, i: (bi, i, 0)),
                 pl.BlockSpec((None, tm, QK_W), lambda bi, i: (bi, i, 0)),
                 pl.BlockSpec((None, V_W, tm), lambda bi, i: (bi, 0, i))]
    out_shape = [jax.ShapeDtypeStruct((b, n, QK_W), BF16),
                 jax.ShapeDtypeStruct((b, n, QK_W), BF16),
                 jax.ShapeDtypeStruct((b, V_W, n), BF16)]
    if emit_k_f32:
        out_specs.append(pl.BlockSpec((None, tm, QK_W), lambda bi, i: (bi, i, 0)))
        out_shape.append(jax.ShapeDtypeStruct((b, n, QK_W), F32))
    return pl.pallas_call(
        functools.partial(_prep_kernel, use_rope=use_rope, emit_k_f32=emit_k_f32),
        grid=(b, n // tm),
        in_specs=in_specs, out_specs=out_specs, out_shape=out_shape,
        compiler_params=_params(("arbitrary", "arbitrary")),
        name="qkv_prep",
    )(*args)


def _attn_kernel(*refs, n_new, n_cache, k_chunk, lam_init):
    q_ref, kn_ref, vnt_ref = refs[:3]
    refs = refs[3:]
    if n_cache:
        kc_ref, vc_ref = refs[:2]
        refs = refs[2:]
    za_ref, gsub_ref, lamp_ref, o_ref, s_ref = refs[:5]
    if n_cache:
        kcs_ref, vcts_ref = refs[5:7]

        @pl.when(pl.program_id(2) == 0)
        def _():
            kcs_ref[...] = kc_ref[...].astype(BF16)
            vcts_ref[...] = vc_ref[...].T.astype(BF16)

    q = q_ref[...]
    lane = lax.broadcasted_iota(jnp.int32, q.shape, 1)
    zero = jnp.zeros_like(q)
    q_comp = (jnp.where(lane < QK_DIM, q, zero), jnp.where(lane >= QK_DIM, q, zero))

    chunks = []
    if n_cache:
        chunks.append((0, n_cache, lambda: kcs_ref[...], lambda: vcts_ref[...]))
    for j in range(n_new // k_chunk):
        chunks.append((n_cache + j * k_chunk, k_chunk,
                       lambda j=j: kn_ref[j * k_chunk:(j + 1) * k_chunk, :],
                       lambda j=j: vnt_ref[:, j * k_chunk:(j + 1) * k_chunk]))

    def scores(c, chunk, m):
        off, rows, k_get, _ = chunk
        s = lax.dot_general(k_get(), q_comp[c], (((1,), (1,)), ((), ())), preferred_element_type=F32)
        s_ref[c, off:off + rows, :] = s
        mc = jnp.max(s, axis=0, keepdims=True)
        return mc if m is None else jnp.maximum(m, mc)

    def weigh(c, chunk, m, l, acc):
        off, rows, _, vt_get = chunk
        p = jnp.exp2(s_ref[c, off:off + rows, :] - m)
        lc = jnp.sum(p, axis=0, keepdims=True)
        d = jnp.dot(vt_get(), p.astype(BF16), preferred_element_type=F32)
        return (lc, d) if l is None else (l + lc, acc + d)

    m0 = m1 = l0 = l1 = acc0 = acc1 = None
    for chunk in chunks:
        m0 = scores(0, chunk, m0)
    for chunk in chunks:
        m1 = scores(1, chunk, m1)
        l0, acc0 = weigh(0, chunk, m0, l0, acc0)
    for chunk in chunks:
        l1, acc1 = weigh(1, chunk, m1, l1, acc1)
    outs = (acc0 / l0, acc1 / l1)

    lp = lamp_ref[...]
    lam = (jnp.exp(jnp.sum(lp[0:1] * lp[1:2], axis=1, keepdims=True))
           - jnp.exp(jnp.sum(lp[2:3] * lp[3:4], axis=1, keepdims=True)) + lam_init)
    o_t = outs[0] - lam * outs[1]
    ms = jnp.mean(o_t * o_t, axis=0, keepdims=True)
    o = (o_t * lax.rsqrt(ms + EPS)).T
    za = za_ref[...].astype(F32)
    o = o * (gsub_ref[...] * (1.0 - lam_init)) * _silu(za)
    o_ref[...] = o.astype(o_ref.dtype)


def _diff_attention(qn, kn, vnt, proj, g_sub_l, lam_params_l, cache_k4, cache_v4, layer_idx, lam_init):
    b, n, _ = qn.shape
    tq = min(512, n)
    k_chunk = min(512, n)
    n_cache = 0 if cache_k4 is None else cache_k4.shape[2]
    in_specs = [pl.BlockSpec((None, tq, HEAD_W), lambda bi, h, i: (bi, i, h)),
                pl.BlockSpec((None, n, HEAD_W), lambda bi, h, i: (bi, 0, h)),
                pl.BlockSpec((None, HEAD_W, n), lambda bi, h, i: (bi, h, 0))]
    args = [qn, kn, vnt]
    scratch = [pltpu.VMEM((2, n_cache + n, tq), F32)]
    if n_cache:
        in_specs += [pl.BlockSpec((None, None, n_cache, HEAD_W), lambda bi, h, i: (bi, layer_idx, 0, h)),
                     pl.BlockSpec((None, None, n_cache, HEAD_W), lambda bi, h, i: (bi, layer_idx, 0, h))]
        args += [cache_k4, cache_v4]
        scratch += [pltpu.VMEM((n_cache, HEAD_W), BF16), pltpu.VMEM((HEAD_W, n_cache), BF16)]
    in_specs += [pl.BlockSpec((None, tq, HEAD_W), lambda bi, h, i: (bi, i, COL_ZA // HEAD_W + h)),
                 pl.BlockSpec((1, V_DIM), lambda bi, h, i: (0, 0)),
                 pl.BlockSpec((4, QK_DIM), lambda bi, h, i: (0, 0))]
    args += [proj, g_sub_l.reshape(1, V_DIM), lam_params_l]
    return pl.pallas_call(
        functools.partial(_attn_kernel, n_new=n, n_cache=n_cache, k_chunk=k_chunk, lam_init=lam_init),
        grid=(b, N_HEADS, n // tq),
        in_specs=in_specs,
        out_specs=pl.BlockSpec((None, tq, HEAD_W), lambda bi, h, i: (bi, i, h)),
        out_shape=jax.ShapeDtypeStruct((b, n, V_W), BF16),
        scratch_shapes=scratch,
        compiler_params=_params(("arbitrary", "arbitrary", "arbitrary")),
        name="diff_attention",
    )(*args)


def _fourier_long_kernel(u_ref, zf_ref, f1_ref, w2_ref, cd_ref, o_ref, x_ref, a_ref, z_ref, *, n):
    side = 64
    halves = F_GROUP_DIM // LANES

    def lane_half(v, hf):
        return v[:, hf * LANES:(hf + 1) * LANES]

    def strided_rows(ref, idx, start):
        return ref[idx, pl.ds(start, side, stride=side), :]

    for hf in range(halves):
        x_ref[hf] = u_ref[:, hf * LANES:(hf + 1) * LANES].astype(F32)
    for b in range(side):
        xb = jnp.concatenate([strided_rows(x_ref, hf, b) for hf in range(halves)], axis=1).astype(BF16)
        ab = jnp.dot(f1_ref[b], xb, preferred_element_type=F32)
        for ri in range(2):
            for hf in range(halves):
                a_ref[2 * ri + hf, b * side:(b + 1) * side, :] = lane_half(ab[ri * side:(ri + 1) * side], hf)
    for c in range(side):
        ac = jnp.concatenate(
            [jnp.concatenate([strided_rows(a_ref, 2 * ri + hf, c) for hf in range(halves)], axis=1)
             for ri in range(2)], axis=0).astype(BF16)
        zc = jnp.dot(w2_ref[...], ac, preferred_element_type=F32)
        for ri in range(2):
            for hf in range(halves):
                z_ref[2 * ri + hf, pl.ds(c, side, stride=side), :] = lane_half(zc[ri * side:(ri + 1) * side], hf)
    rows = 512
    cd = cd_ref[...]
    for r in range(n // rows):
        sl = slice(r * rows, (r + 1) * rows)
        zr = jnp.concatenate([z_ref[hf, sl, :] for hf in range(halves)], axis=1).astype(BF16)
        zi = jnp.concatenate([z_ref[2 + hf, sl, :] for hf in range(halves)], axis=1).astype(BF16)
        y = (jnp.dot(zr, cd[:F_GROUP_DIM], preferred_element_type=F32)
             + jnp.dot(zi, cd[F_GROUP_DIM:], preferred_element_type=F32))
        o_ref[sl, :] = (y * _silu(zf_ref[sl, :].astype(F32))).astype(o_ref.dtype)


def _fourier_short_kernel(u_ref, zf_ref, fn_ref, cd_ref, o_ref, *, n):
    cd = cd_ref[...]
    u = u_ref[...]
    tc = jnp.dot(u, cd[:F_GROUP_DIM], preferred_element_type=F32).astype(BF16)
    ts = jnp.dot(u, cd[F_GROUP_DIM:], preferred_element_type=F32).astype(BF16)
    fn = fn_ref[...]
    y = (jnp.dot(fn[:, :n], tc, preferred_element_type=F32)
         + jnp.dot(fn[:, n:], ts, preferred_element_type=F32))
    o_ref[...] = (y * _silu(zf_ref[...].astype(F32))).astype(o_ref.dtype)


def _fourier_branch(proj):
    b, n, _ = proj.shape
    gc = F_GROUP_DIM
    scale = 1.0 / math.sqrt(n * gc)
    cd = _channel_dft(scale)
    u_spec = pl.BlockSpec((None, n, gc), lambda bi, g: (bi, 0, COL_UF // gc + g))
    zf_spec = pl.BlockSpec((None, n, gc), lambda bi, g: (bi, 0, COL_ZF // gc + g))
    cd_spec = pl.BlockSpec((2 * gc, gc), lambda bi, g: (0, 0))
    out_spec = pl.BlockSpec((None, n, gc), lambda bi, g: (bi, 0, g))
    out_shape = jax.ShapeDtypeStruct((b, n, F_DIM), BF16)
    if n == 64 * 64:
        f1, w2 = _long_dft_tables(n)
        return pl.pallas_call(
            functools.partial(_fourier_long_kernel, n=n),
            grid=(b, F_GROUPS),
            in_specs=[u_spec, zf_spec,
                      pl.BlockSpec((64, 128, 64), lambda bi, g: (0, 0, 0)),
                      pl.BlockSpec((128, 128), lambda bi, g: (0, 0)),
                      cd_spec],
            out_specs=out_spec, out_shape=out_shape,
            scratch_shapes=[pltpu.VMEM((gc // LANES, n, LANES), F32),
                            pltpu.VMEM((2 * gc // LANES, n, LANES), F32),
                            pltpu.VMEM((2 * gc // LANES, n, LANES), F32)],
            compiler_params=_params(("arbitrary", "arbitrary")),
            name="fourier_long",
        )(proj, proj, f1, w2, cd)
    fn = _short_dft_table(n)
    return pl.pallas_call(
        functools.partial(_fourier_short_kernel, n=n),
        grid=(b, F_GROUPS),
        in_specs=[u_spec, zf_spec, pl.BlockSpec((n, 2 * n), lambda bi, g: (0, 0)), cd_spec],
        out_specs=out_spec, out_shape=out_shape,
        compiler_params=_params(("arbitrary", "arbitrary")),
        name="fourier_short",
    )(proj, proj, fn, cd)


def _outproj_kernel(yf_ref, ya_ref, gf_ref, ga_ref, x_ref, mod_ref, wf_ref, wa_ref, wo_ref, o_ref, *, d):
    f = jnp.dot(yf_ref[...], wf_ref[...], preferred_element_type=F32)
    a = jnp.dot(ya_ref[...], wa_ref[...], preferred_element_type=F32)
    merged = jax.nn.sigmoid(gf_ref[...].astype(F32)) * f + jax.nn.sigmoid(ga_ref[...].astype(F32)) * a
    out = jnp.dot(merged.astype(BF16), wo_ref[...], preferred_element_type=F32)
    gate = mod_ref[:, 2 * d:3 * d]
    o_ref[...] = x_ref[...] + gate * out


def _out_projection(yf2d, ya2d, proj2d, x2d, mod_l, wf, wa, wo, *, rows_per_batch, mod_row0, per_batch_mod):
    m, d = x2d.shape
    tm = min(256, rows_per_batch)
    blocks_per_batch = rows_per_batch // tm
    if per_batch_mod:
        mod_map = lambda i: (mod_row0 + i // blocks_per_batch, 0, 0)
    else:
        mod_map = lambda i: (mod_row0, 0, 0)
    const = lambda shape: pl.BlockSpec(shape, lambda i: (0, 0), pipeline_mode=pl.Buffered(1))
    return pl.pallas_call(
        functools.partial(_outproj_kernel, d=d),
        grid=(m // tm,),
        in_specs=[pl.BlockSpec((tm, F_DIM), lambda i: (i, 0)),
                  pl.BlockSpec((tm, V_W), lambda i: (i, 0)),
                  pl.BlockSpec((tm, d), lambda i: (i, COL_GF // d)),
                  pl.BlockSpec((tm, d), lambda i: (i, COL_GF // d + 1)),
                  pl.BlockSpec((tm, d), lambda i: (i, 0)),
                  pl.BlockSpec((None, 1, 3 * d), mod_map),
                  const((F_DIM, d)), const((V_W, d)), const((d, d))],
        out_specs=pl.BlockSpec((tm, d), lambda i: (i, 0)),
        out_shape=jax.ShapeDtypeStruct((m, d), F32),
        compiler_params=_params(("arbitrary",)),
        name="out_projection",
    )(yf2d, ya2d, proj2d, proj2d, x2d, mod_l.reshape(MOD_ROWS, 1, 3 * d), wf, wa, wo)


def _layer(x, mod_l, weights_l, lam_init, *, mod_row0, per_batch_mod, rope_tabs, cache_k4, cache_v4, layer_idx,
           emit_cache):
    w_in, w_fproj, w_aproj, w_out, g_norm, g_q, g_k, g_sub, lam_params = weights_l
    b, n, d = x.shape
    x2d = x.reshape(b * n, d)
    proj2d = _in_projection(x2d, mod_l, g_norm, w_in, rows_per_batch=n, mod_row0=mod_row0,
                            per_batch_mod=per_batch_mod)
    proj = proj2d.reshape(b, n, -1)
    prep = _qkv_prep(proj, g_q, g_k, rope_tabs, emit_k_f32=emit_cache)
    qn, kn, vnt = prep[:3]
    ya = _diff_attention(qn, kn, vnt, proj, g_sub, lam_params, cache_k4, cache_v4, layer_idx, lam_init)
    yf = _fourier_branch(proj)
    y2d = _out_projection(yf.reshape(b * n, F_DIM), ya.reshape(b * n, V_W), proj2d, x2d, mod_l,
                          w_fproj, w_aproj, w_out, rows_per_batch=n, mod_row0=mod_row0,
                          per_batch_mod=per_batch_mod)
    y = y2d.reshape(b, n, d)
    if emit_cache:
        k_new = prep[3].reshape(b, n, N_HEADS, 2, QK_DIM)
        v_new = proj[:, :, COL_V:COL_V + V_W].astype(F32).reshape(b, n, N_HEADS, V_DIM)
        return y, k_new, v_new
    return y, None, None


def kernel(x_prompt, x_sample, c, cache_k, cache_v, c_ctx, w_in, w_fproj, w_aproj, w_out, w_mod, b_mod, g_norm,
           g_q, g_k, g_sub, lam_q1, lam_k1, lam_q2, lam_k2):
    depth = w_in.shape[0]
    d = x_prompt.shape[-1]
    dec_b, dec_n = x_sample.shape[0], x_sample.shape[1]
    assert 1 + dec_b <= MOD_ROWS

    cvecs = jnp.concatenate([c_ctx[None, :], c, jnp.zeros((MOD_ROWS - 1 - dec_b, d), F32)], axis=0)
    mod = _modulation(cvecs, w_mod, b_mod)

    w_in_b, w_fproj_b, w_aproj_b, w_out_b = (w.astype(BF16) for w in (w_in, w_fproj, w_aproj, w_out))
    lam_params = jnp.stack([lam_q1, lam_k1, lam_q2, lam_k2], axis=1)
    rope_tabs = _rope_tables(dec_n)
    past = cache_k.shape[2]
    cache_k4 = cache_k.reshape(dec_b, depth, past, QK_W)
    cache_v4 = cache_v.reshape(dec_b, depth, past, V_W)

    y_p, y_s = x_prompt, x_sample
    ks_out, vs_out = [], []
    for l in range(depth):
        lam_init = 0.8 - 0.6 * math.exp(-0.3 * l)
        weights_l = (w_in_b[l], w_fproj_b[l], w_aproj_b[l], w_out_b[l], g_norm[l], g_q[l], g_k[l], g_sub[l],
                     lam_params[l])
        y_p, k_l, v_l = _layer(y_p, mod[l], weights_l, lam_init, mod_row0=0, per_batch_mod=False, rope_tabs=None,
                               cache_k4=None, cache_v4=None, layer_idx=l, emit_cache=True)
        ks_out.append(k_l)
        vs_out.append(v_l)
        y_s, _, _ = _layer(y_s, mod[l], weights_l, lam_init, mod_row0=1, per_batch_mod=True, rope_tabs=rope_tabs,
                           cache_k4=cache_k4, cache_v4=cache_v4, layer_idx=l, emit_cache=False)
    return (y_p, y_s, jnp.stack(ks_out, axis=1), jnp.stack(vs_out, axis=1))
```

```python
import functools
import math

import numpy as np
import jax
import jax.numpy as jnp
from jax import lax
from jax.experimental import pallas as pl
from jax.experimental.pallas import tpu as pltpu

F32 = jnp.float32
BF16 = jnp.bfloat16

GRID_W = 64
F_GROUPS = 4
F_GROUP_DIM = 256
F_DIM = F_GROUPS * F_GROUP_DIM
N_HEADS = 8
QK_DIM = 64
V_DIM = 2 * QK_DIM
QK_W = N_HEADS * 2 * QK_DIM
V_W = N_HEADS * V_DIM
HEAD_W = 128
LANES = 128
SUBLANES = 8
ROPE_BASE = 10000.0
EPS = 1e-6

COL_UF, COL_ZF, COL_Q, COL_K, COL_V, COL_ZA, COL_GF = 0, 1024, 2048, 3072, 4096, 5120, 6144

MOD_ROWS = 8
VMEM_LIMIT = 56 * 1024 * 1024


def _params(sem, vmem=VMEM_LIMIT):
    return pltpu.CompilerParams(dimension_semantics=sem, vmem_limit_bytes=vmem)


def _silu(x):
    return x * jax.nn.sigmoid(x)


def _rope_tables(n_tokens):
    rows = n_tokens // GRID_W
    r = np.repeat(np.arange(rows, dtype=np.float32), GRID_W)
    c = np.tile(np.arange(GRID_W, dtype=np.float32), rows)
    half = QK_DIM // 2
    inv = (1.0 / (np.float32(ROPE_BASE) ** (np.arange(half // 2, dtype=np.float32) * np.float32(2.0) / np.float32(half)))).astype(np.float32)
    ang_r = (r[:, None] * inv[None, :]).astype(np.float32).astype(np.float64)
    ang_c = (c[:, None] * inv[None, :]).astype(np.float32).astype(np.float64)
    cos = np.zeros((n_tokens, QK_DIM)); sa = np.zeros((n_tokens, QK_DIM)); sb = np.zeros((n_tokens, QK_DIM))
    q = half // 2
    for sec, ang in ((0, ang_r), (1, ang_c)):
        base = sec * half
        cos[:, base:base + q] = np.cos(ang); cos[:, base + q:base + half] = np.cos(ang)
        sb[:, base:base + q] = -np.sin(ang)
        sa[:, base + q:base + half] = np.sin(ang)
    tile = lambda t: jnp.asarray(np.tile(t, (1, HEAD_W // QK_DIM)), F32)
    return tile(cos), tile(sa), tile(sb)


def _group_mean_matrix():
    m = np.kron(np.eye(256 // QK_DIM), np.full((QK_DIM, QK_DIM), 1.0 / QK_DIM))
    return jnp.asarray(m, BF16)


def _channel_dft(scale):
    c = np.arange(F_GROUP_DIM)
    ang = 2.0 * np.pi * np.outer(c, c) / F_GROUP_DIM
    return jnp.asarray(np.concatenate([np.cos(ang), np.sin(ang)], axis=0) * scale, BF16)


def _long_dft_tables(n):
    side, blk = 64, SUBLANES
    a = np.arange(side)
    ang = 2.0 * np.pi * np.outer(a, a) / side
    cs, sn = np.cos(ang), np.sin(ang)
    eye = np.eye(blk)
    f64 = np.stack([cs, -sn])
    m1 = np.einsum('rca,lm->lrcam', f64, eye).reshape(blk * 2 * side, side * blk)
    angt = 2.0 * np.pi * np.outer(a, a).reshape(-1) / n
    tw = np.stack([np.cos(angt), np.sin(angt)])[:, :, None] * np.ones((1, 1, LANES))
    w2 = np.stack([np.stack([cs, sn]), np.stack([-sn, cs])])
    m2 = np.einsum('rsdb,lm->rdlsbm', w2, eye).reshape(2 * side * blk, 2 * side * blk)
    return jnp.asarray(m1, BF16), jnp.asarray(tw, F32), jnp.asarray(m2, BF16)


def _short_dft_table(n):
    t = np.arange(n)
    ang = 2.0 * np.pi * np.outer(t, t) / n
    return jnp.asarray(np.concatenate([np.cos(ang), -np.sin(ang)], axis=1), BF16)


def _mod_kernel(c_ref, w_ref, b_ref, o_ref):
    s = _silu(c_ref[...])
    o_ref[...] = jnp.dot(s.astype(BF16), w_ref[...].astype(BF16), preferred_element_type=F32) + b_ref[...]


def _modulation(cvecs, w_mod, b_mod):
    depth, d, d3 = w_mod.shape
    tn = 1024 if d3 % 1024 == 0 else d3
    return pl.pallas_call(
        _mod_kernel,
        grid=(depth, d3 // tn),
        in_specs=[pl.BlockSpec((MOD_ROWS, d), lambda l, j: (0, 0)),
                  pl.BlockSpec((None, d, tn), lambda l, j: (l, 0, j)),
                  pl.BlockSpec((None, 1, tn), lambda l, j: (l, 0, j))],
        out_specs=pl.BlockSpec((None, MOD_ROWS, tn), lambda l, j: (l, 0, j)),
        out_shape=jax.ShapeDtypeStruct((depth, MOD_ROWS, d3), F32),
        compiler_params=_params(("arbitrary", "arbitrary")),
        name="modulation",
    )(cvecs, w_mod, b_mod.reshape(depth, 1, d3))


def _inproj_kernel(x_ref, mod_ref, g_ref, w_ref, o_ref, h_ref, *, d):
    @pl.when(pl.program_id(1) == 0)
    def _():
        x = x_ref[...]
        ms = jnp.mean(x * x, axis=-1, keepdims=True)
        shift = mod_ref[:, 0:d]
        scale = mod_ref[:, d:2 * d]
        h = x * lax.rsqrt(ms + EPS) * (g_ref[...] * (1.0 + scale)) + shift
        h_ref[...] = h.astype(BF16)

    o_ref[...] = jnp.dot(h_ref[...], w_ref[...], preferred_element_type=F32).astype(o_ref.dtype)


def _in_projection(x2d, mod_l, g_norm_l, w_in_l, *, rows_per_batch, mod_row0, per_batch_mod):
    m, d = x2d.shape
    n_cols = w_in_l.shape[1]
    tm = min(1024, rows_per_batch if per_batch_mod else m)
    tn = 1024 if n_cols % 1024 == 0 else 512
    blocks_per_batch = rows_per_batch // tm
    if per_batch_mod:
        mod_map = lambda i, j: (mod_row0 + i // blocks_per_batch, 0, 0)
    else:
        mod_map = lambda i, j: (mod_row0, 0, 0)
    return pl.pallas_call(
        functools.partial(_inproj_kernel, d=d),
        grid=(m // tm, n_cols // tn),
        in_specs=[pl.BlockSpec((tm, d), lambda i, j: (i, 0)),
                  pl.BlockSpec((None, 1, 3 * d), mod_map),
                  pl.BlockSpec((1, d), lambda i, j: (0, 0)),
                  pl.BlockSpec((d, tn), lambda i, j: (0, j))],
        out_specs=pl.BlockSpec((tm, tn), lambda i, j: (i, j)),
        out_shape=jax.ShapeDtypeStruct((m, n_cols), BF16),
        scratch_shapes=[pltpu.VMEM((tm, d), BF16)],
        compiler_params=_params(("arbitrary", "arbitrary")),
        name="in_projection",
    )(x2d, mod_l.reshape(MOD_ROWS, 1, 3 * d), g_norm_l.reshape(1, d), w_in_l)


def _prep_kernel(*refs, use_rope, emit_k_f32):
    q_ref, k_ref, v_ref, gq_ref, gk_ref, gm_ref = refs[:6]
    refs = refs[6:]
    if use_rope:
        cos_ref, sa_ref, sb_ref = refs[:3]
        refs = refs[3:]
    qo_ref, ko_ref, vt_ref = refs[:3]
    kf_ref = refs[3] if emit_k_f32 else None

    def group_norm(x, g):
        sq = (x * x).astype(BF16)
        parts = [jnp.dot(sq[:, i * 256:(i + 1) * 256], gm_ref[...], preferred_element_type=F32)
                 for i in range(QK_W // 256)]
        ms = jnp.concatenate(parts, axis=1)
        return x * lax.rsqrt(ms + EPS) * g

    def rope(x):
        cos, sa, sb = cos_ref[...], sa_ref[...], sb_ref[...]
        outs = []
        for i in range(QK_W // HEAD_W):
            xc = x[:, i * HEAD_W:(i + 1) * HEAD_W]
            outs.append(xc * cos + pltpu.roll(xc, 16, 1) * sa + pltpu.roll(xc, HEAD_W - 16, 1) * sb)
        return jnp.concatenate(outs, axis=1)

    q = group_norm(q_ref[...].astype(F32), gq_ref[...])
    k = group_norm(k_ref[...].astype(F32), gk_ref[...])
    if use_rope:
        q, k = rope(q), rope(k)
    qo_ref[...] = (q * (QK_DIM ** -0.5 * math.log2(math.e))).astype(BF16)
    ko_ref[...] = k.astype(BF16)
    if emit_k_f32:
        kf_ref[...] = k
    for h in range(N_HEADS):
        vh = v_ref[:, h * HEAD_W:(h + 1) * HEAD_W].astype(F32)
        vt_ref[h * HEAD_W:(h + 1) * HEAD_W, :] = vh.T.astype(BF16)


def _qkv_prep(proj, g_q_l, g_k_l, rope_tabs, *, emit_k_f32):
    b, n, _ = proj.shape
    tm = min(512, n)
    use_rope = rope_tabs is not None
    reps = QK_W // QK_DIM
    in_specs = [pl.BlockSpec((None, tm, QK_W), lambda bi, i: (bi, i, COL_Q // QK_W)),
                pl.BlockSpec((None, tm, QK_W), lambda bi, i: (bi, i, COL_K // QK_W)),
                pl.BlockSpec((None, tm, V_W), lambda bi, i: (bi, i, COL_V // V_W)),
                pl.BlockSpec((1, QK_W), lambda bi, i: (0, 0)),
                pl.BlockSpec((1, QK_W), lambda bi, i: (0, 0)),
                pl.BlockSpec((256, 256), lambda bi, i: (0, 0))]
    args = [proj, proj, proj, jnp.tile(g_q_l, reps).reshape(1, QK_W), jnp.tile(g_k_l, reps).reshape(1, QK_W),
            _group_mean_matrix()]
    if use_rope:
        in_specs += [pl.BlockSpec((tm, HEAD_W), lambda bi, i: (i, 0))] * 3
        args += list(rope_tabs)
    out_specs = [pl.BlockSpec((None, tm, QK_W), lambda bi, i: (bi, i, 0)),
                 pl.BlockSpec((None, tm, QK_W), lambda bi, i: (bi, i, 0)),
                 pl.BlockSpec((None, V_W, tm), lambda bi, i: (bi, 0, i))]
    out_shape = [jax.ShapeDtypeStruct((b, n, QK_W), BF16),
                 jax.ShapeDtypeStruct((b, n, QK_W), BF16),
                 jax.ShapeDtypeStruct((b, V_W, n), BF16)]
    if emit_k_f32:
        out_specs.append(pl.BlockSpec((None, tm, QK_W), lambda bi, i: (bi, i, 0)))
        out_shape.append(jax.ShapeDtypeStruct((b, n, QK_W), F32))
    return pl.pallas_call(
        functools.partial(_prep_kernel, use_rope=use_rope, emit_k_f32=emit_k_f32),
        grid=(b, n // tm),
        in_specs=in_specs, out_specs=out_specs, out_shape=out_shape,
        compiler_params=_params(("arbitrary", "arbitrary")),
        name="qkv_prep",
    )(*args)


ATTN_QUERY_TILE = 256
ATTN_TILES_PER_STEP = 4
ATTN_KEY_CHUNK = 512
ATTN_HEADS_PER_STEP_SHORT = 4


def _attn_kernel(*refs, n_new, n_cache, k_chunk, lam_init, tq, n_tiles, heads):
    q_ref, kn_ref, vnt_ref = refs[:3]
    refs = refs[3:]
    if n_cache:
        kc_ref, vc_ref = refs[:2]
        refs = refs[2:]
    za_ref, gsub_ref, lamp_ref, o_ref, s_ref = refs[:5]
    if n_cache:
        assert heads == 1
        kall_ref, vcts_ref = refs[5:7]

        @pl.when(pl.program_id(2) == 0)
        def _():
            kall_ref[0:n_cache, :] = kc_ref[...].astype(BF16)
            kall_ref[n_cache:n_cache + n_new, :] = kn_ref[...]
            vcts_ref[...] = vc_ref[...].T.astype(BF16)
    else:
        kall_ref = kn_ref

    def head_lanes(hl):
        return slice(hl * HEAD_W, (hl + 1) * HEAD_W)

    chunks = []
    if n_cache:
        chunks.append((0, n_cache, lambda hl: vcts_ref[...]))
    for j in range(n_new // k_chunk):
        chunks.append((n_cache + j * k_chunk, k_chunk,
                       lambda hl, j=j: vnt_ref[head_lanes(hl), j * k_chunk:(j + 1) * k_chunk]))

    lp = lamp_ref[...]
    lam = (jnp.exp(jnp.sum(lp[0:1] * lp[1:2], axis=1, keepdims=True))
           - jnp.exp(jnp.sum(lp[2:3] * lp[3:4], axis=1, keepdims=True)) + lam_init)

    def masked_q(hl, tile, c):
        q = q_ref[tile * tq:(tile + 1) * tq, head_lanes(hl)]
        lane = lax.broadcasted_iota(jnp.int32, q.shape, 1)
        keep = (lane < QK_DIM) if c == 0 else (lane >= QK_DIM)
        return jnp.where(keep, q, jnp.zeros_like(q))

    def scores(slot, hl, qc, chunk, m):
        off, rows = chunk[:2]
        s = lax.dot_general(kall_ref[off:off + rows, head_lanes(hl)], qc, (((1,), (1,)), ((), ())),
                            preferred_element_type=F32)
        s_ref[slot, off:off + rows, :] = s
        mc = jnp.max(s, axis=0, keepdims=True)
        return mc if m is None else jnp.maximum(m, mc)

    def weigh(slot, hl, chunk, m, state):
        off, rows, vt_get = chunk
        p = jnp.exp2(s_ref[slot, off:off + rows, :] - m)
        lc = jnp.sum(p, axis=0, keepdims=True)
        d = jnp.dot(vt_get(hl), p.astype(BF16), preferred_element_type=F32)
        return (lc, d) if state is None else (state[0] + lc, state[1] + d)

    def finish(hl, tile, state0, state1):
        o_t = state0[1] / state0[0] - lam * (state1[1] / state1[0])
        ms = jnp.mean(o_t * o_t, axis=0, keepdims=True)
        o = (o_t * lax.rsqrt(ms + EPS)).T
        rows = slice(tile * tq, (tile + 1) * tq)
        za = za_ref[rows, head_lanes(hl)].astype(F32)
        o = o * (gsub_ref[...] * (1.0 - lam_init)) * _silu(za)
        o_ref[rows, head_lanes(hl)] = o.astype(o_ref.dtype)

    jobs = [(hl, t, c) for hl in range(heads) for t in range(n_tiles) for c in range(2)]
    maxes, accs = {}, {}
    for i in range(len(jobs) + 1):
        qc = masked_q(*jobs[i]) if i < len(jobs) else None
        m_new = acc = None
        for chunk in chunks:
            if i < len(jobs):
                m_new = scores(i % 2, jobs[i][0], qc, chunk, m_new)
            if i > 0:
                acc = weigh((i - 1) % 2, jobs[i - 1][0], chunk, maxes[i - 1], acc)
        maxes[i] = m_new
        if i > 0:
            hl, tile, c = jobs[i - 1]
            accs[c] = acc
            if c == 1:
                finish(hl, tile, accs[0], accs[1])


def _diff_attention(qn, kn, vnt, proj, g_sub_l, lam_params_l, cache_k4, cache_v4, layer_idx, lam_init):
    b, n, _ = qn.shape
    tq = min(ATTN_QUERY_TILE, n)
    n_tiles = min(ATTN_TILES_PER_STEP, n // tq)
    tb = tq * n_tiles
    k_chunk = min(ATTN_KEY_CHUNK, n)
    n_cache = 0 if cache_k4 is None else cache_k4.shape[2]
    heads = 1 if (n_cache or n // tb > 1) else ATTN_HEADS_PER_STEP_SHORT
    hw = heads * HEAD_W
    in_specs = [pl.BlockSpec((None, tb, hw), lambda bi, h, i: (bi, i, h)),
                pl.BlockSpec((None, n, hw), lambda bi, h, i: (bi, 0, h)),
                pl.BlockSpec((None, hw, n), lambda bi, h, i: (bi, h, 0))]
    args = [qn, kn, vnt]
    scratch = [pltpu.VMEM((2, n_cache + n, tq), F32)]
    if n_cache:
        in_specs += [pl.BlockSpec((None, None, n_cache, HEAD_W), lambda bi, h, i: (bi, layer_idx, 0, h)),
                     pl.BlockSpec((None, None, n_cache, HEAD_W), lambda bi, h, i: (bi, layer_idx, 0, h))]
        args += [cache_k4, cache_v4]
        scratch += [pltpu.VMEM((n_cache + n, HEAD_W), BF16), pltpu.VMEM((HEAD_W, n_cache), BF16)]
    in_specs += [pl.BlockSpec((None, tb, hw), lambda bi, h, i: (bi, i, COL_ZA // hw + h)),
                 pl.BlockSpec((1, V_DIM), lambda bi, h, i: (0, 0)),
                 pl.BlockSpec((4, QK_DIM), lambda bi, h, i: (0, 0))]
    args += [proj, g_sub_l.reshape(1, V_DIM), lam_params_l]
    return pl.pallas_call(
        functools.partial(_attn_kernel, n_new=n, n_cache=n_cache, k_chunk=k_chunk, lam_init=lam_init, tq=tq,
                          n_tiles=n_tiles, heads=heads),
        grid=(b, N_HEADS // heads, n // tb),
        in_specs=in_specs,
        out_specs=pl.BlockSpec((None, tb, hw), lambda bi, h, i: (bi, i, h)),
        out_shape=jax.ShapeDtypeStruct((b, n, V_W), BF16),
        scratch_shapes=scratch,
        compiler_params=_params(("arbitrary", "arbitrary", "arbitrary")),
        name="diff_attention",
    )(*args)


def _fourier_long_kernel(u_ref, zf_ref, m1_ref, tw_ref, m2_ref, cd_ref, o_ref, x_ref, a_ref, z_ref, *, n):
    side, blk = 64, SUBLANES
    gc = F_GROUP_DIM
    x_ref[...] = u_ref[...].astype(F32).reshape(side, side, gc)
    for bb in range(side // blk):
        xb = x_ref[:, bb * blk:(bb + 1) * blk, :].reshape(side * blk, gc).astype(BF16)
        ab = jnp.dot(m1_ref[...], xb, preferred_element_type=F32)
        for bl in range(blk):
            b = bb * blk + bl
            ar = ab[bl * 2 * side:bl * 2 * side + side]
            ai = ab[bl * 2 * side + side:(bl + 1) * 2 * side]
            ct = jnp.concatenate([tw_ref[0, b * side:(b + 1) * side, :]] * (gc // LANES), axis=1)
            st = jnp.concatenate([tw_ref[1, b * side:(b + 1) * side, :]] * (gc // LANES), axis=1)
            a_ref[0, b] = ar * ct + ai * st
            a_ref[1, b] = ai * ct - ar * st
    for cb in range(side // blk):
        ac = jnp.concatenate([a_ref[ri, :, cb * blk:(cb + 1) * blk, :].reshape(side * blk, gc) for ri in range(2)],
                             axis=0).astype(BF16)
        zc = jnp.dot(m2_ref[...], ac, preferred_element_type=F32)
        for ri in range(2):
            z_ref[ri, :, cb * blk:(cb + 1) * blk, :] = zc[ri * side * blk:(ri + 1) * side * blk].reshape(side, blk, gc)
    d_rows = 8
    cd = cd_ref[...]
    for r in range(side // d_rows):
        sl = slice(r * d_rows * side, (r + 1) * d_rows * side)
        zr = z_ref[0, r * d_rows:(r + 1) * d_rows].reshape(d_rows * side, gc).astype(BF16)
        zi = z_ref[1, r * d_rows:(r + 1) * d_rows].reshape(d_rows * side, gc).astype(BF16)
        y = (jnp.dot(zr, cd[:gc], preferred_element_type=F32)
             + jnp.dot(zi, cd[gc:], preferred_element_type=F32))
        o_ref[sl, :] = (y * _silu(zf_ref[sl, :].astype(F32))).astype(o_ref.dtype)


def _fourier_short_kernel(u_ref, zf_ref, fn_ref, cd_ref, o_ref, *, n):
    cd = cd_ref[...]
    u = u_ref[...]
    tc = jnp.dot(u, cd[:F_GROUP_DIM], preferred_element_type=F32).astype(BF16)
    ts = jnp.dot(u, cd[F_GROUP_DIM:], preferred_element_type=F32).astype(BF16)
    fn = fn_ref[...]
    y = (jnp.dot(fn[:, :n], tc, preferred_element_type=F32)
         + jnp.dot(fn[:, n:], ts, preferred_element_type=F32))
    o_ref[...] = (y * _silu(zf_ref[...].astype(F32))).astype(o_ref.dtype)


def _fourier_branch(proj):
    b, n, _ = proj.shape
    gc = F_GROUP_DIM
    scale = 1.0 / math.sqrt(n * gc)
    cd = _channel_dft(scale)
    u_spec = pl.BlockSpec((None, n, gc), lambda bi, g: (bi, 0, COL_UF // gc + g))
    zf_spec = pl.BlockSpec((None, n, gc), lambda bi, g: (bi, 0, COL_ZF // gc + g))
    cd_spec = pl.BlockSpec((2 * gc, gc), lambda bi, g: (0, 0))
    out_spec = pl.BlockSpec((None, n, gc), lambda bi, g: (bi, 0, g))
    out_shape = jax.ShapeDtypeStruct((b, n, F_DIM), BF16)
    if n == 64 * 64:
        m1, tw, m2 = _long_dft_tables(n)
        const = lambda a: pl.BlockSpec(a.shape, lambda bi, g: (0,) * a.ndim, pipeline_mode=pl.Buffered(1))
        return pl.pallas_call(
            functools.partial(_fourier_long_kernel, n=n),
            grid=(b, F_GROUPS),
            in_specs=[u_spec, zf_spec, const(m1), const(tw), const(m2), cd_spec],
            out_specs=out_spec, out_shape=out_shape,
            scratch_shapes=[pltpu.VMEM((64, 64, gc), F32),
                            pltpu.VMEM((2, 64, 64, gc), F32),
                            pltpu.VMEM((2, 64, 64, gc), F32)],
            compiler_params=_params(("arbitrary", "arbitrary")),
            name="fourier_long",
        )(proj, proj, m1, tw, m2, cd)
    fn = _short_dft_table(n)
    return pl.pallas_call(
        functools.partial(_fourier_short_kernel, n=n),
        grid=(b, F_GROUPS),
        in_specs=[u_spec, zf_spec, pl.BlockSpec((n, 2 * n), lambda bi, g: (0, 0)), cd_spec],
        out_specs=out_spec, out_shape=out_shape,
        compiler_params=_params(("arbitrary", "arbitrary")),
        name="fourier_short",
    )(proj, proj, fn, cd)


def _outproj_kernel(yf_ref, ya_ref, gf_ref, ga_ref, x_ref, mod_ref, wf_ref, wa_ref, wo_ref, o_ref, *, d):
    f = jnp.dot(yf_ref[...], wf_ref[...], preferred_element_type=F32)
    a = jnp.dot(ya_ref[...], wa_ref[...], preferred_element_type=F32)
    merged = jax.nn.sigmoid(gf_ref[...].astype(F32)) * f + jax.nn.sigmoid(ga_ref[...].astype(F32)) * a
    out = jnp.dot(merged.astype(BF16), wo_ref[...], preferred_element_type=F32)
    gate = mod_ref[:, 2 * d:3 * d]
    o_ref[...] = x_ref[...] + gate * out


def _out_projection(yf2d, ya2d, proj2d, x2d, mod_l, wf, wa, wo, *, rows_per_batch, mod_row0, per_batch_mod):
    m, d = x2d.shape
    tm = min(256, rows_per_batch)
    blocks_per_batch = rows_per_batch // tm
    if per_batch_mod:
        mod_map = lambda i: (mod_row0 + i // blocks_per_batch, 0, 0)
    else:
        mod_map = lambda i: (mod_row0, 0, 0)
    const = lambda shape: pl.BlockSpec(shape, lambda i: (0, 0), pipeline_mode=pl.Buffered(1))
    return pl.pallas_call(
        functools.partial(_outproj_kernel, d=d),
        grid=(m // tm,),
        in_specs=[pl.BlockSpec((tm, F_DIM), lambda i: (i, 0)),
                  pl.BlockSpec((tm, V_W), lambda i: (i, 0)),
                  pl.BlockSpec((tm, d), lambda i: (i, COL_GF // d)),
                  pl.BlockSpec((tm, d), lambda i: (i, COL_GF // d + 1)),
                  pl.BlockSpec((tm, d), lambda i: (i, 0)),
                  pl.BlockSpec((None, 1, 3 * d), mod_map),
                  const((F_DIM, d)), const((V_W, d)), const((d, d))],
        out_specs=pl.BlockSpec((tm, d), lambda i: (i, 0)),
        out_shape=jax.ShapeDtypeStruct((m, d), F32),
        compiler_params=_params(("arbitrary",)),
        name="out_projection",
    )(yf2d, ya2d, proj2d, proj2d, x2d, mod_l.reshape(MOD_ROWS, 1, 3 * d), wf, wa, wo)


def _layer(x, mod_l, weights_l, lam_init, *, mod_row0, per_batch_mod, rope_tabs, cache_k4, cache_v4, layer_idx,
           emit_cache):
    w_in, w_fproj, w_aproj, w_out, g_norm, g_q, g_k, g_sub, lam_params = weights_l
    b, n, d = x.shape
    x2d = x.reshape(b * n, d)
    proj2d = _in_projection(x2d, mod_l, g_norm, w_in, rows_per_batch=n, mod_row0=mod_row0,
                            per_batch_mod=per_batch_mod)
    proj = proj2d.reshape(b, n, -1)
    prep = _qkv_prep(proj, g_q, g_k, rope_tabs, emit_k_f32=emit_cache)
    qn, kn, vnt = prep[:3]
    ya = _diff_attention(qn, kn, vnt, proj, g_sub, lam_params, cache_k4, cache_v4, layer_idx, lam_init)
    yf = _fourier_branch(proj)
    y2d = _out_projection(yf.reshape(b * n, F_DIM), ya.reshape(b * n, V_W), proj2d, x2d, mod_l,
                          w_fproj, w_aproj, w_out, rows_per_batch=n, mod_row0=mod_row0,
                          per_batch_mod=per_batch_mod)
    y = y2d.reshape(b, n, d)
    if emit_cache:
        k_new = prep[3].reshape(b, n, N_HEADS, 2, QK_DIM)
        v_new = proj[:, :, COL_V:COL_V + V_W].astype(F32).reshape(b, n, N_HEADS, V_DIM)
        return y, k_new, v_new
    return y, None, None


def kernel(x_prompt, x_sample, c, cache_k, cache_v, c_ctx, w_in, w_fproj, w_aproj, w_out, w_mod, b_mod, g_norm,
           g_q, g_k, g_sub, lam_q1, lam_k1, lam_q2, lam_k2):
    depth = w_in.shape[0]
    d = x_prompt.shape[-1]
    dec_b, dec_n = x_sample.shape[0], x_sample.shape[1]
    assert 1 + dec_b <= MOD_ROWS

    cvecs = jnp.concatenate([c_ctx[None, :], c, jnp.zeros((MOD_ROWS - 1 - dec_b, d), F32)], axis=0)
    mod = _modulation(cvecs, w_mod, b_mod)

    w_in_b, w_fproj_b, w_aproj_b, w_out_b = (w.astype(BF16) for w in (w_in, w_fproj, w_aproj, w_out))
    lam_params = jnp.stack([lam_q1, lam_k1, lam_q2, lam_k2], axis=1)
    rope_tabs = _rope_tables(dec_n)
    past = cache_k.shape[2]
    cache_k4 = cache_k.reshape(dec_b, depth, past, QK_W)
    cache_v4 = cache_v.reshape(dec_b, depth, past, V_W)

    y_p, y_s = x_prompt, x_sample
    ks_out, vs_out = [], []
    for l in range(depth):
        lam_init = 0.8 - 0.6 * math.exp(-0.3 * l)
        weights_l = (w_in_b[l], w_fproj_b[l], w_aproj_b[l], w_out_b[l], g_norm[l], g_q[l], g_k[l], g_sub[l],
                     lam_params[l])
        y_p, k_l, v_l = _layer(y_p, mod[l], weights_l, lam_init, mod_row0=0, per_batch_mod=False, rope_tabs=None,
                               cache_k4=None, cache_v4=None, layer_idx=l, emit_cache=True)
        ks_out.append(k_l)
        vs_out.append(v_l)
        y_s, _, _ = _layer(y_s, mod[l], weights_l, lam_init, mod_row0=1, per_batch_mod=True, rope_tabs=rope_tabs,
                           cache_k4=cache_k4, cache_v4=cache_v4, layer_idx=l, emit_cache=False)
    return (y_p, y_s, jnp.stack(ks_out, axis=1), jnp.stack(vs_out, axis=1))
```

```python
import functools
import math

import numpy as np
import jax
import jax.numpy as jnp
from jax import lax
from jax.experimental import pallas as pl
from jax.experimental.pallas import tpu as pltpu

F32 = jnp.float32
BF16 = jnp.bfloat16

GRID_W = 64
F_GROUPS = 4
F_GROUP_DIM = 256
F_DIM = F_GROUPS * F_GROUP_DIM
N_HEADS = 8
QK_DIM = 64
V_DIM = 2 * QK_DIM
QK_W = N_HEADS * 2 * QK_DIM
V_W = N_HEADS * V_DIM
HEAD_W = 128
LANES = 128
SUBLANES = 8
ROPE_BASE = 10000.0
EPS = 1e-6

COL_UF, COL_ZF, COL_Q, COL_K, COL_V, COL_ZA, COL_GF = 0, 1024, 2048, 3072, 4096, 5120, 6144

MOD_ROWS = 8
VMEM_LIMIT = 56 * 1024 * 1024


def _params(sem, vmem=VMEM_LIMIT):
    return pltpu.CompilerParams(dimension_semantics=sem, vmem_limit_bytes=vmem)


def _silu(x):
    return x * jax.nn.sigmoid(x)


def _rope_tables(n_tokens):
    rows = n_tokens // GRID_W
    r = np.repeat(np.arange(rows, dtype=np.float32), GRID_W)
    c = np.tile(np.arange(GRID_W, dtype=np.float32), rows)
    half = QK_DIM // 2
    inv = (1.0 / (np.float32(ROPE_BASE) ** (np.arange(half // 2, dtype=np.float32) * np.float32(2.0) / np.float32(half)))).astype(np.float32)
    ang_r = (r[:, None] * inv[None, :]).astype(np.float32).astype(np.float64)
    ang_c = (c[:, None] * inv[None, :]).astype(np.float32).astype(np.float64)
    cos = np.zeros((n_tokens, QK_DIM)); sa = np.zeros((n_tokens, QK_DIM)); sb = np.zeros((n_tokens, QK_DIM))
    q = half // 2
    for sec, ang in ((0, ang_r), (1, ang_c)):
        base = sec * half
        cos[:, base:base + q] = np.cos(ang); cos[:, base + q:base + half] = np.cos(ang)
        sb[:, base:base + q] = -np.sin(ang)
        sa[:, base + q:base + half] = np.sin(ang)
    tile = lambda t: jnp.asarray(np.tile(t, (1, HEAD_W // QK_DIM)), F32)
    return tile(cos), tile(sa), tile(sb)


def _group_mean_matrix():
    m = np.kron(np.eye(256 // QK_DIM), np.full((QK_DIM, QK_DIM), 1.0 / QK_DIM))
    return jnp.asarray(m, BF16)


def _channel_dft(scale):
    c = np.arange(F_GROUP_DIM)
    ang = 2.0 * np.pi * np.outer(c, c) / F_GROUP_DIM
    return jnp.asarray(np.concatenate([np.cos(ang), np.sin(ang)], axis=0) * scale, BF16)


def _long_dft_tables(n):
    side, blk = 64, SUBLANES
    a = np.arange(side)
    ang = 2.0 * np.pi * np.outer(a, a) / side
    cs, sn = np.cos(ang), np.sin(ang)
    eye = np.eye(blk)
    f64 = np.stack([cs, -sn])
    m1 = np.einsum('rca,lm->lrcam', f64, eye).reshape(blk * 2 * side, side * blk)
    angt = 2.0 * np.pi * np.outer(a, a).reshape(-1) / n
    tw = np.stack([np.cos(angt), np.sin(angt)])[:, :, None] * np.ones((1, 1, LANES))
    w2 = np.stack([np.stack([cs, sn]), np.stack([-sn, cs])])
    m2 = np.einsum('rsdb,lm->rdlsbm', w2, eye).reshape(2 * side * blk, 2 * side * blk)
    return jnp.asarray(m1, BF16), jnp.asarray(tw, F32), jnp.asarray(m2, BF16)


def _short_dft_table(n):
    t = np.arange(n)
    ang = 2.0 * np.pi * np.outer(t, t) / n
    return jnp.asarray(np.concatenate([np.cos(ang), -np.sin(ang)], axis=1), BF16)


def _mod_kernel(c_ref, w_ref, b_ref, o_ref):
    s = _silu(c_ref[...])
    o_ref[...] = jnp.dot(s.astype(BF16), w_ref[...].astype(BF16), preferred_element_type=F32) + b_ref[...]


def _modulation(cvecs, w_mod, b_mod):
    depth, d, d3 = w_mod.shape
    tn = 1024 if d3 % 1024 == 0 else d3
    return pl.pallas_call(
        _mod_kernel,
        grid=(depth, d3 // tn),
        in_specs=[pl.BlockSpec((MOD_ROWS, d), lambda l, j: (0, 0)),
                  pl.BlockSpec((None, d, tn), lambda l, j: (l, 0, j)),
                  pl.BlockSpec((None, 1, tn), lambda l, j: (l, 0, j))],
        out_specs=pl.BlockSpec((None, MOD_ROWS, tn), lambda l, j: (l, 0, j)),
        out_shape=jax.ShapeDtypeStruct((depth, MOD_ROWS, d3), F32),
        compiler_params=_params(("arbitrary", "arbitrary")),
        name="modulation",
    )(cvecs, w_mod, b_mod.reshape(depth, 1, d3))


def _inproj_kernel(x_ref, mod_ref, g_ref, w_ref, o_ref, h_ref, *, d):
    @pl.when(pl.program_id(1) == 0)
    def _():
        x = x_ref[...]
        ms = jnp.mean(x * x, axis=-1, keepdims=True)
        shift = mod_ref[:, 0:d]
        scale = mod_ref[:, d:2 * d]
        h = x * lax.rsqrt(ms + EPS) * (g_ref[...] * (1.0 + scale)) + shift
        h_ref[...] = h.astype(BF16)

    o_ref[...] = jnp.dot(h_ref[...], w_ref[...], preferred_element_type=F32).astype(o_ref.dtype)


def _in_projection(x2d, mod_l, g_norm_l, w_in_l, *, rows_per_batch, mod_row0, per_batch_mod):
    m, d = x2d.shape
    n_cols = w_in_l.shape[1]
    tm = min(1024, rows_per_batch if per_batch_mod else m)
    tn = next(t for t in (2048, 1024, 512) if n_cols % t == 0)
    blocks_per_batch = rows_per_batch // tm
    if per_batch_mod:
        mod_map = lambda i, j: (mod_row0 + i // blocks_per_batch, 0, 0)
    else:
        mod_map = lambda i, j: (mod_row0, 0, 0)
    return pl.pallas_call(
        functools.partial(_inproj_kernel, d=d),
        grid=(m // tm, n_cols // tn),
        in_specs=[pl.BlockSpec((tm, d), lambda i, j: (i, 0)),
                  pl.BlockSpec((None, 1, 3 * d), mod_map),
                  pl.BlockSpec((1, d), lambda i, j: (0, 0)),
                  pl.BlockSpec((d, tn), lambda i, j: (0, j))],
        out_specs=pl.BlockSpec((tm, tn), lambda i, j: (i, j)),
        out_shape=jax.ShapeDtypeStruct((m, n_cols), BF16),
        scratch_shapes=[pltpu.VMEM((tm, d), BF16)],
        compiler_params=_params(("arbitrary", "arbitrary")),
        name="in_projection",
    )(x2d, mod_l.reshape(MOD_ROWS, 1, 3 * d), g_norm_l.reshape(1, d), w_in_l)


def _prep_kernel(*refs, use_rope, emit_k_f32):
    q_ref, k_ref, v_ref, gq_ref, gk_ref, gm_ref = refs[:6]
    refs = refs[6:]
    if use_rope:
        cos_ref, sa_ref, sb_ref = refs[:3]
        refs = refs[3:]
    qo_ref, ko_ref, vt_ref = refs[:3]
    kf_ref, vf_ref = refs[3:5] if emit_k_f32 else (None, None)

    def group_norm(x, g):
        sq = (x * x).astype(BF16)
        parts = [jnp.dot(sq[:, i * 256:(i + 1) * 256], gm_ref[...], preferred_element_type=F32)
                 for i in range(QK_W // 256)]
        ms = jnp.concatenate(parts, axis=1)
        return x * lax.rsqrt(ms + EPS) * g

    def rope(x):
        cos, sa, sb = cos_ref[...], sa_ref[...], sb_ref[...]
        outs = []
        for i in range(QK_W // HEAD_W):
            xc = x[:, i * HEAD_W:(i + 1) * HEAD_W]
            outs.append(xc * cos + pltpu.roll(xc, 16, 1) * sa + pltpu.roll(xc, HEAD_W - 16, 1) * sb)
        return jnp.concatenate(outs, axis=1)

    q = group_norm(q_ref[...].astype(F32), gq_ref[...])
    k = group_norm(k_ref[...].astype(F32), gk_ref[...])
    if use_rope:
        q, k = rope(q), rope(k)
    qo_ref[...] = (q * (QK_DIM ** -0.5 * math.log2(math.e))).astype(BF16)
    ko_ref[...] = k.astype(BF16)
    if emit_k_f32:
        kf_ref[...] = k
        vf_ref[...] = v_ref[...].astype(F32)
    for h in range(N_HEADS):
        vh = v_ref[:, h * HEAD_W:(h + 1) * HEAD_W].astype(F32)
        vt_ref[h * HEAD_W:(h + 1) * HEAD_W, :] = vh.T.astype(BF16)


def _qkv_prep(proj, g_q_l, g_k_l, rope_tabs, *, emit_k_f32):
    b, n, _ = proj.shape
    tm = min(512, n)
    use_rope = rope_tabs is not None
    reps = QK_W // QK_DIM
    in_specs = [pl.BlockSpec((None, tm, QK_W), lambda bi, i: (bi, i, COL_Q // QK_W)),
                pl.BlockSpec((None, tm, QK_W), lambda bi, i: (bi, i, COL_K // QK_W)),
                pl.BlockSpec((None, tm, V_W), lambda bi, i: (bi, i, COL_V // V_W)),
                pl.BlockSpec((1, QK_W), lambda bi, i: (0, 0)),
                pl.BlockSpec((1, QK_W), lambda bi, i: (0, 0)),
                pl.BlockSpec((256, 256), lambda bi, i: (0, 0))]
    args = [proj, proj, proj, jnp.tile(g_q_l, reps).reshape(1, QK_W), jnp.tile(g_k_l, reps).reshape(1, QK_W),
            _group_mean_matrix()]
    if use_rope:
        in_specs += [pl.BlockSpec((tm, HEAD_W), lambda bi, i: (i, 0))] * 3
        args += list(rope_tabs)
    out_specs = [pl.BlockSpec((None, tm, QK_W), lambda bi, i: (bi, i, 0)),
                 pl.BlockSpec((None, tm, QK_W), lambda bi, i: (bi, i, 0)),
                 pl.BlockSpec((None, V_W, tm), lambda bi, i: (bi, 0, i))]
    out_shape = [jax.ShapeDtypeStruct((b, n, QK_W), BF16),
                 jax.ShapeDtypeStruct((b, n, QK_W), BF16),
                 jax.ShapeDtypeStruct((b, V_W, n), BF16)]
    if emit_k_f32:
        out_specs += [pl.BlockSpec((None, tm, QK_W), lambda bi, i: (bi, i, 0)),
                      pl.BlockSpec((None, tm, V_W), lambda bi, i: (bi, i, 0))]
        out_shape += [jax.ShapeDtypeStruct((b, n, QK_W), F32), jax.ShapeDtypeStruct((b, n, V_W), F32)]
    return pl.pallas_call(
        functools.partial(_prep_kernel, use_rope=use_rope, emit_k_f32=emit_k_f32),
        grid=(b, n // tm),
        in_specs=in_specs, out_specs=out_specs, out_shape=out_shape,
        compiler_params=_params(("arbitrary", "arbitrary")),
        name="qkv_prep",
    )(*args)


ATTN_QUERY_TILE = 256
ATTN_TILES_PER_STEP = 4
ATTN_KEY_CHUNK = 512
ATTN_HEADS_PER_STEP_SHORT = 4


def _attn_kernel(*refs, n_new, n_cache, k_chunk, lam_init, tq, n_tiles, heads):
    q_ref, kn_ref, vnt_ref = refs[:3]
    refs = refs[3:]
    if n_cache:
        kc_ref, vc_ref = refs[:2]
        refs = refs[2:]
    za_ref, gsub_ref, lamp_ref, o_ref = refs[:4]
    s_refs = refs[4:6]
    if n_cache:
        assert heads == 1
        kall_ref, vcts_ref = refs[6:8]

        @pl.when(pl.program_id(2) == 0)
        def _():
            kall_ref[0:n_cache, :] = kc_ref[...].astype(BF16)
            kall_ref[n_cache:n_cache + n_new, :] = kn_ref[...]
            vcts_ref[...] = vc_ref[...].T.astype(BF16)
    else:
        kall_ref = kn_ref

    def head_lanes(hl):
        return slice(hl * HEAD_W, (hl + 1) * HEAD_W)

    chunks = []
    if n_cache:
        chunks.append((0, n_cache, lambda hl: vcts_ref[...]))
    for j in range(n_new // k_chunk):
        chunks.append((n_cache + j * k_chunk, k_chunk,
                       lambda hl, j=j: vnt_ref[head_lanes(hl), j * k_chunk:(j + 1) * k_chunk]))

    lp = lamp_ref[...]
    lam = (jnp.exp(jnp.sum(lp[0:1] * lp[1:2], axis=1, keepdims=True))
           - jnp.exp(jnp.sum(lp[2:3] * lp[3:4], axis=1, keepdims=True)) + lam_init)

    def masked_q(hl, tile, c):
        q = q_ref[tile * tq:(tile + 1) * tq, head_lanes(hl)]
        lane = lax.broadcasted_iota(jnp.int32, q.shape, 1)
        keep = (lane < QK_DIM) if c == 0 else (lane >= QK_DIM)
        return jnp.where(keep, q, jnp.zeros_like(q))

    def scores(slot, hl, qc, chunk, m):
        off, rows = chunk[:2]
        s = lax.dot_general(kall_ref[off:off + rows, head_lanes(hl)], qc, (((1,), (1,)), ((), ())),
                            preferred_element_type=F32)
        s_refs[slot][off:off + rows, :] = s
        mc = jnp.max(s, axis=0, keepdims=True)
        return mc if m is None else jnp.maximum(m, mc)

    def weigh(slot, hl, chunk, m, state):
        off, rows, vt_get = chunk
        p = jnp.exp2(s_refs[slot][off:off + rows, :] - m)
        lc = jnp.sum(p, axis=0, keepdims=True)
        d = jnp.dot(vt_get(hl), p.astype(BF16), preferred_element_type=F32)
        return (lc, d) if state is None else (state[0] + lc, state[1] + d)

    def finish(hl, tile, state0, state1):
        o_t = state0[1] * (1.0 / state0[0]) - state1[1] * (lam / state1[0])
        ms = jnp.mean(o_t * o_t, axis=0, keepdims=True)
        o = (o_t * lax.rsqrt(ms + EPS)).T
        rows = slice(tile * tq, (tile + 1) * tq)
        za = za_ref[rows, head_lanes(hl)].astype(F32)
        o = o * (gsub_ref[...] * (1.0 - lam_init)) * _silu(za)
        o_ref[rows, head_lanes(hl)] = o.astype(o_ref.dtype)

    jobs = [(hl, t, c) for hl in range(heads) for t in range(n_tiles) for c in range(2)]
    maxes, accs = {}, {}
    for i in range(len(jobs) + 1):
        qc = masked_q(*jobs[i]) if i < len(jobs) else None
        m_new = acc = None
        for chunk in chunks:
            if i < len(jobs):
                m_new = scores(i % 2, jobs[i][0], qc, chunk, m_new)
            if i > 0:
                acc = weigh((i - 1) % 2, jobs[i - 1][0], chunk, maxes[i - 1], acc)
        maxes[i] = m_new
        if i > 0:
            hl, tile, c = jobs[i - 1]
            accs[c] = acc
            if c == 1:
                finish(hl, tile, accs[0], accs[1])


def _diff_attention(qn, kn, vnt, proj, g_sub_l, lam_params_l, cache_k4, cache_v4, layer_idx, lam_init):
    b, n, _ = qn.shape
    tq = min(ATTN_QUERY_TILE, n)
    n_tiles = min(ATTN_TILES_PER_STEP, n // tq)
    tb = tq * n_tiles
    k_chunk = min(ATTN_KEY_CHUNK, n)
    n_cache = 0 if cache_k4 is None else cache_k4.shape[2]
    heads = 1 if (n_cache or n // tb > 1) else ATTN_HEADS_PER_STEP_SHORT
    hw = heads * HEAD_W
    in_specs = [pl.BlockSpec((None, tb, hw), lambda bi, h, i: (bi, i, h)),
                pl.BlockSpec((None, n, hw), lambda bi, h, i: (bi, 0, h)),
                pl.BlockSpec((None, hw, n), lambda bi, h, i: (bi, h, 0))]
    args = [qn, kn, vnt]
    scratch = [pltpu.VMEM((n_cache + n, tq), F32), pltpu.VMEM((n_cache + n, tq), F32)]
    if n_cache:
        in_specs += [pl.BlockSpec((None, None, n_cache, HEAD_W), lambda bi, h, i: (bi, layer_idx, 0, h)),
                     pl.BlockSpec((None, None, n_cache, HEAD_W), lambda bi, h, i: (bi, layer_idx, 0, h))]
        args += [cache_k4, cache_v4]
        scratch += [pltpu.VMEM((n_cache + n, HEAD_W), BF16), pltpu.VMEM((HEAD_W, n_cache), BF16)]
    in_specs += [pl.BlockSpec((None, tb, hw), lambda bi, h, i: (bi, i, COL_ZA // hw + h)),
                 pl.BlockSpec((1, V_DIM), lambda bi, h, i: (0, 0)),
                 pl.BlockSpec((4, QK_DIM), lambda bi, h, i: (0, 0))]
    args += [proj, g_sub_l.reshape(1, V_DIM), lam_params_l]
    return pl.pallas_call(
        functools.partial(_attn_kernel, n_new=n, n_cache=n_cache, k_chunk=k_chunk, lam_init=lam_init, tq=tq,
                          n_tiles=n_tiles, heads=heads),
        grid=(b, N_HEADS // heads, n // tb),
        in_specs=in_specs,
        out_specs=pl.BlockSpec((None, tb, hw), lambda bi, h, i: (bi, i, h)),
        out_shape=jax.ShapeDtypeStruct((b, n, V_W), BF16),
        scratch_shapes=scratch,
        compiler_params=_params(("arbitrary", "arbitrary", "arbitrary")),
        name="diff_attention",
    )(*args)


def _fourier_long_kernel(u_ref, zf_ref, m1_ref, tw_ref, m2_ref, cd_ref, o_ref, x_ref, a_ref, z_ref, *, n):
    side, blk = 64, SUBLANES
    gc = F_GROUP_DIM
    x_ref[...] = u_ref[...].astype(F32).reshape(side, side, gc)
    for bb in range(side // blk):
        xb = x_ref[:, bb * blk:(bb + 1) * blk, :].reshape(side * blk, gc).astype(BF16)
        ab = jnp.dot(m1_ref[...], xb, preferred_element_type=F32)
        for bl in range(blk):
            b = bb * blk + bl
            ar = ab[bl * 2 * side:bl * 2 * side + side]
            ai = ab[bl * 2 * side + side:(bl + 1) * 2 * side]
            ct = jnp.concatenate([tw_ref[0, b * side:(b + 1) * side, :]] * (gc // LANES), axis=1)
            st = jnp.concatenate([tw_ref[1, b * side:(b + 1) * side, :]] * (gc // LANES), axis=1)
            a_ref[0, b] = ar * ct + ai * st
            a_ref[1, b] = ai * ct - ar * st
    for cb in range(side // blk):
        ac = jnp.concatenate([a_ref[ri, :, cb * blk:(cb + 1) * blk, :].reshape(side * blk, gc) for ri in range(2)],
                             axis=0).astype(BF16)
        zc = jnp.dot(m2_ref[...], ac, preferred_element_type=F32)
        for ri in range(2):
            z_ref[ri, :, cb * blk:(cb + 1) * blk, :] = zc[ri * side * blk:(ri + 1) * side * blk].reshape(side, blk, gc)
    d_rows = 8
    cd = cd_ref[...]
    for r in range(side // d_rows):
        sl = slice(r * d_rows * side, (r + 1) * d_rows * side)
        zr = z_ref[0, r * d_rows:(r + 1) * d_rows].reshape(d_rows * side, gc).astype(BF16)
        zi = z_ref[1, r * d_rows:(r + 1) * d_rows].reshape(d_rows * side, gc).astype(BF16)
        y = (jnp.dot(zr, cd[:gc], preferred_element_type=F32)
             + jnp.dot(zi, cd[gc:], preferred_element_type=F32))
        o_ref[sl, :] = (y * _silu(zf_ref[sl, :].astype(F32))).astype(o_ref.dtype)


def _fourier_short_kernel(u_ref, zf_ref, fn_ref, cd_ref, o_ref, *, n):
    cd = cd_ref[...]
    u = u_ref[...]
    fn = fn_ref[...]
    for g in range(F_GROUPS):
        cols = slice(g * F_GROUP_DIM, (g + 1) * F_GROUP_DIM)
        ug = u[:, cols]
        tc = jnp.dot(ug, cd[:F_GROUP_DIM], preferred_element_type=F32).astype(BF16)
        ts = jnp.dot(ug, cd[F_GROUP_DIM:], preferred_element_type=F32).astype(BF16)
        y = (jnp.dot(fn[:, :n], tc, preferred_element_type=F32)
             + jnp.dot(fn[:, n:], ts, preferred_element_type=F32))
        o_ref[:, cols] = (y * _silu(zf_ref[:, cols].astype(F32))).astype(o_ref.dtype)


def _fourier_branch(proj):
    b, n, _ = proj.shape
    gc = F_GROUP_DIM
    scale = 1.0 / math.sqrt(n * gc)
    cd = _channel_dft(scale)
    u_spec = pl.BlockSpec((None, n, gc), lambda bi, g: (bi, 0, COL_UF // gc + g))
    zf_spec = pl.BlockSpec((None, n, gc), lambda bi, g: (bi, 0, COL_ZF // gc + g))
    cd_spec = pl.BlockSpec((2 * gc, gc), lambda bi, g: (0, 0))
    out_spec = pl.BlockSpec((None, n, gc), lambda bi, g: (bi, 0, g))
    out_shape = jax.ShapeDtypeStruct((b, n, F_DIM), BF16)
    if n == 64 * 64:
        m1, tw, m2 = _long_dft_tables(n)
        const = lambda a: pl.BlockSpec(a.shape, lambda bi, g: (0,) * a.ndim, pipeline_mode=pl.Buffered(1))
        return pl.pallas_call(
            functools.partial(_fourier_long_kernel, n=n),
            grid=(b, F_GROUPS),
            in_specs=[u_spec, zf_spec, const(m1), const(tw), const(m2), cd_spec],
            out_specs=out_spec, out_shape=out_shape,
            scratch_shapes=[pltpu.VMEM((64, 64, gc), F32),
                            pltpu.VMEM((2, 64, 64, gc), F32),
                            pltpu.VMEM((2, 64, 64, gc), F32)],
            compiler_params=_params(("arbitrary", "arbitrary")),
            name="fourier_long",
        )(proj, proj, m1, tw, m2, cd)
    fn = _short_dft_table(n)
    return pl.pallas_call(
        functools.partial(_fourier_short_kernel, n=n),
        grid=(b,),
        in_specs=[pl.BlockSpec((None, n, F_DIM), lambda bi: (bi, 0, COL_UF // F_DIM)),
                  pl.BlockSpec((None, n, F_DIM), lambda bi: (bi, 0, COL_ZF // F_DIM)),
                  pl.BlockSpec((n, 2 * n), lambda bi: (0, 0)),
                  pl.BlockSpec((2 * gc, gc), lambda bi: (0, 0))],
        out_specs=pl.BlockSpec((None, n, F_DIM), lambda bi: (bi, 0, 0)), out_shape=out_shape,
        compiler_params=_params(("arbitrary",)),
        name="fourier_short",
    )(proj, proj, fn, cd)


def _outproj_kernel(yf_ref, ya_ref, gf_ref, ga_ref, x_ref, mod_ref, wf_ref, wa_ref, wo_ref, o_ref, *, d):
    f = jnp.dot(yf_ref[...], wf_ref[...], preferred_element_type=F32)
    a = jnp.dot(ya_ref[...], wa_ref[...], preferred_element_type=F32)
    merged = jax.nn.sigmoid(gf_ref[...].astype(F32)) * f + jax.nn.sigmoid(ga_ref[...].astype(F32)) * a
    out = jnp.dot(merged.astype(BF16), wo_ref[...], preferred_element_type=F32)
    gate = mod_ref[:, 2 * d:3 * d]
    o_ref[...] = x_ref[...] + gate * out


def _out_projection(yf2d, ya2d, proj2d, x2d, mod_l, wf, wa, wo, *, rows_per_batch, mod_row0, per_batch_mod):
    m, d = x2d.shape
    tm = min(256, rows_per_batch)
    blocks_per_batch = rows_per_batch // tm
    if per_batch_mod:
        mod_map = lambda i: (mod_row0 + i // blocks_per_batch, 0, 0)
    else:
        mod_map = lambda i: (mod_row0, 0, 0)
    const = lambda shape: pl.BlockSpec(shape, lambda i: (0, 0), pipeline_mode=pl.Buffered(1))
    return pl.pallas_call(
        functools.partial(_outproj_kernel, d=d),
        grid=(m // tm,),
        in_specs=[pl.BlockSpec((tm, F_DIM), lambda i: (i, 0)),
                  pl.BlockSpec((tm, V_W), lambda i: (i, 0)),
                  pl.BlockSpec((tm, d), lambda i: (i, COL_GF // d)),
                  pl.BlockSpec((tm, d), lambda i: (i, COL_GF // d + 1)),
                  pl.BlockSpec((tm, d), lambda i: (i, 0)),
                  pl.BlockSpec((None, 1, 3 * d), mod_map),
                  const((F_DIM, d)), const((V_W, d)), const((d, d))],
        out_specs=pl.BlockSpec((tm, d), lambda i: (i, 0)),
        out_shape=jax.ShapeDtypeStruct((m, d), F32),
        compiler_params=_params(("arbitrary",)),
        name="out_projection",
    )(yf2d, ya2d, proj2d, proj2d, x2d, mod_l.reshape(MOD_ROWS, 1, 3 * d), wf, wa, wo)


def _layer(x, mod_l, weights_l, lam_init, *, mod_row0, per_batch_mod, rope_tabs, cache_k4, cache_v4, layer_idx,
           emit_cache):
    w_in, w_fproj, w_aproj, w_out, g_norm, g_q, g_k, g_sub, lam_params = weights_l
    b, n, d = x.shape
    x2d = x.reshape(b * n, d)
    proj2d = _in_projection(x2d, mod_l, g_norm, w_in, rows_per_batch=n, mod_row0=mod_row0,
                            per_batch_mod=per_batch_mod)
    proj = proj2d.reshape(b, n, -1)
    prep = _qkv_prep(proj, g_q, g_k, rope_tabs, emit_k_f32=emit_cache)
    qn, kn, vnt = prep[:3]
    ya = _diff_attention(qn, kn, vnt, proj, g_sub, lam_params, cache_k4, cache_v4, layer_idx, lam_init)
    yf = _fourier_branch(proj)
    y2d = _out_projection(yf.reshape(b * n, F_DIM), ya.reshape(b * n, V_W), proj2d, x2d, mod_l,
                          w_fproj, w_aproj, w_out, rows_per_batch=n, mod_row0=mod_row0,
                          per_batch_mod=per_batch_mod)
    y = y2d.reshape(b, n, d)
    if emit_cache:
        k_new = prep[3].reshape(b, n, N_HEADS, 2, QK_DIM)
        v_new = prep[4].reshape(b, n, N_HEADS, V_DIM)
        return y, k_new, v_new
    return y, None, None


def kernel(x_prompt, x_sample, c, cache_k, cache_v, c_ctx, w_in, w_fproj, w_aproj, w_out, w_mod, b_mod, g_norm,
           g_q, g_k, g_sub, lam_q1, lam_k1, lam_q2, lam_k2):
    depth = w_in.shape[0]
    d = x_prompt.shape[-1]
    dec_b, dec_n = x_sample.shape[0], x_sample.shape[1]
    assert 1 + dec_b <= MOD_ROWS

    cvecs = jnp.concatenate([c_ctx[None, :], c, jnp.zeros((MOD_ROWS - 1 - dec_b, d), F32)], axis=0)
    mod = _modulation(cvecs, w_mod, b_mod)

    w_in_b, w_fproj_b, w_aproj_b, w_out_b = (w.astype(BF16) for w in (w_in, w_fproj, w_aproj, w_out))
    lam_params = jnp.stack([lam_q1, lam_k1, lam_q2, lam_k2], axis=1)
    rope_tabs = _rope_tables(dec_n)
    past = cache_k.shape[2]
    cache_k4 = cache_k.reshape(dec_b, depth, past, QK_W)
    cache_v4 = cache_v.reshape(dec_b, depth, past, V_W)

    y_p, y_s = x_prompt, x_sample
    ks_out, vs_out = [], []
    for l in range(depth):
        lam_init = 0.8 - 0.6 * math.exp(-0.3 * l)
        weights_l = (w_in_b[l], w_fproj_b[l], w_aproj_b[l], w_out_b[l], g_norm[l], g_q[l], g_k[l], g_sub[l],
                     lam_params[l])
        y_p, k_l, v_l = _layer(y_p, mod[l], weights_l, lam_init, mod_row0=0, per_batch_mod=False, rope_tabs=None,
                               cache_k4=None, cache_v4=None, layer_idx=l, emit_cache=True)
        ks_out.append(k_l)
        vs_out.append(v_l)
        y_s, _, _ = _layer(y_s, mod[l], weights_l, lam_init, mod_row0=1, per_batch_mod=True, rope_tabs=rope_tabs,
                           cache_k4=cache_k4, cache_v4=cache_v4, layer_idx=l, emit_cache=False)
    return (y_p, y_s, jnp.stack(ks_out, axis=1), jnp.stack(vs_out, axis=1))
```

```python
import functools
import math

import numpy as np
import jax
import jax.numpy as jnp
from jax import lax
from jax.experimental import pallas as pl
from jax.experimental.pallas import tpu as pltpu

F32 = jnp.float32
BF16 = jnp.bfloat16

GRID_W = 64
F_GROUPS = 4
F_GROUP_DIM = 256
F_DIM = F_GROUPS * F_GROUP_DIM
N_HEADS = 8
QK_DIM = 64
V_DIM = 2 * QK_DIM
QK_W = N_HEADS * 2 * QK_DIM
V_W = N_HEADS * V_DIM
HEAD_W = 128
LANES = 128
SUBLANES = 8
ROPE_BASE = 10000.0
EPS = 1e-6

COL_UF, COL_ZF, COL_Q, COL_K, COL_V, COL_ZA, COL_GF = 0, 1024, 2048, 3072, 4096, 5120, 6144

MOD_ROWS = 8
VMEM_LIMIT = 56 * 1024 * 1024


def _params(sem, vmem=VMEM_LIMIT):
    return pltpu.CompilerParams(dimension_semantics=sem, vmem_limit_bytes=vmem)


def _silu(x):
    return x * jax.nn.sigmoid(x)


def _rope_tables(n_tokens):
    rows = n_tokens // GRID_W
    r = np.repeat(np.arange(rows, dtype=np.float32), GRID_W)
    c = np.tile(np.arange(GRID_W, dtype=np.float32), rows)
    half = QK_DIM // 2
    inv = (1.0 / (np.float32(ROPE_BASE) ** (np.arange(half // 2, dtype=np.float32) * np.float32(2.0) / np.float32(half)))).astype(np.float32)
    ang_r = (r[:, None] * inv[None, :]).astype(np.float32).astype(np.float64)
    ang_c = (c[:, None] * inv[None, :]).astype(np.float32).astype(np.float64)
    cos = np.zeros((n_tokens, QK_DIM)); sa = np.zeros((n_tokens, QK_DIM)); sb = np.zeros((n_tokens, QK_DIM))
    q = half // 2
    for sec, ang in ((0, ang_r), (1, ang_c)):
        base = sec * half
        cos[:, base:base + q] = np.cos(ang); cos[:, base + q:base + half] = np.cos(ang)
        sb[:, base:base + q] = -np.sin(ang)
        sa[:, base + q:base + half] = np.sin(ang)
    tile = lambda t: jnp.asarray(np.tile(t, (1, HEAD_W // QK_DIM)), F32)
    return tile(cos), tile(sa), tile(sb)


def _group_mean_matrix():
    m = np.kron(np.eye(256 // QK_DIM), np.full((QK_DIM, QK_DIM), 1.0 / QK_DIM))
    return jnp.asarray(m, BF16)


def _channel_dft(scale):
    c = np.arange(F_GROUP_DIM)
    ang = 2.0 * np.pi * np.outer(c, c) / F_GROUP_DIM
    return jnp.asarray(np.concatenate([np.cos(ang), np.sin(ang)], axis=0) * scale, BF16)


def _long_dft_tables(n):
    side, blk = 64, SUBLANES
    a = np.arange(side)
    ang = 2.0 * np.pi * np.outer(a, a) / side
    cs, sn = np.cos(ang), np.sin(ang)
    eye = np.eye(blk)
    f64 = np.stack([cs, -sn])
    m1 = np.einsum('rca,lm->lrcam', f64, eye).reshape(blk * 2 * side, side * blk)
    angt = 2.0 * np.pi * np.outer(a, a).reshape(-1) / n
    tw = np.stack([np.cos(angt), np.sin(angt)])[:, :, None] * np.ones((1, 1, LANES))
    w2 = np.stack([np.stack([cs, sn]), np.stack([-sn, cs])])
    m2 = np.einsum('rsdb,lm->rdlsbm', w2, eye).reshape(2 * side * blk, 2 * side * blk)
    return jnp.asarray(m1, BF16), jnp.asarray(tw, F32), jnp.asarray(m2, BF16)


def _short_dft_table(n):
    t = np.arange(n)
    ang = 2.0 * np.pi * np.outer(t, t) / n
    return jnp.asarray(np.concatenate([np.cos(ang), -np.sin(ang)], axis=1), BF16)


def _mod_kernel(c_ref, w_ref, b_ref, o_ref):
    s = _silu(c_ref[...])
    o_ref[...] = jnp.dot(s.astype(BF16), w_ref[...].astype(BF16), preferred_element_type=F32) + b_ref[...]


def _modulation(cvecs, w_mod, b_mod):
    depth, d, d3 = w_mod.shape
    tn = 1024 if d3 % 1024 == 0 else d3
    return pl.pallas_call(
        _mod_kernel,
        grid=(depth, d3 // tn),
        in_specs=[pl.BlockSpec((MOD_ROWS, d), lambda l, j: (0, 0)),
                  pl.BlockSpec((None, d, tn), lambda l, j: (l, 0, j)),
                  pl.BlockSpec((None, 1, tn), lambda l, j: (l, 0, j))],
        out_specs=pl.BlockSpec((None, MOD_ROWS, tn), lambda l, j: (l, 0, j)),
        out_shape=jax.ShapeDtypeStruct((depth, MOD_ROWS, d3), F32),
        compiler_params=_params(("arbitrary", "arbitrary")),
        name="modulation",
    )(cvecs, w_mod, b_mod.reshape(depth, 1, d3))


def _inproj_kernel(x_ref, mod_ref, g_ref, w_ref, o_ref, h_ref, *, d):
    @pl.when(pl.program_id(1) == 0)
    def _():
        x = x_ref[...]
        ms = jnp.mean(x * x, axis=-1, keepdims=True)
        shift = mod_ref[:, 0:d]
        scale = mod_ref[:, d:2 * d]
        h = x * lax.rsqrt(ms + EPS) * (g_ref[...] * (1.0 + scale)) + shift
        h_ref[...] = h.astype(BF16)

    o_ref[...] = jnp.dot(h_ref[...], w_ref[...], preferred_element_type=F32).astype(o_ref.dtype)


def _in_projection(x2d, mod_l, g_norm_l, w_in_l, *, rows_per_batch, mod_row0, per_batch_mod):
    m, d = x2d.shape
    n_cols = w_in_l.shape[1]
    tm = min(1024, rows_per_batch if per_batch_mod else m)
    tn = next(t for t in (2048, 1024, 512) if n_cols % t == 0)
    blocks_per_batch = rows_per_batch // tm
    if per_batch_mod:
        mod_map = lambda i, j: (mod_row0 + i // blocks_per_batch, 0, 0)
    else:
        mod_map = lambda i, j: (mod_row0, 0, 0)
    return pl.pallas_call(
        functools.partial(_inproj_kernel, d=d),
        grid=(m // tm, n_cols // tn),
        in_specs=[pl.BlockSpec((tm, d), lambda i, j: (i, 0)),
                  pl.BlockSpec((None, 1, 3 * d), mod_map),
                  pl.BlockSpec((1, d), lambda i, j: (0, 0)),
                  pl.BlockSpec((d, tn), lambda i, j: (0, j))],
        out_specs=pl.BlockSpec((tm, tn), lambda i, j: (i, j)),
        out_shape=jax.ShapeDtypeStruct((m, n_cols), BF16),
        scratch_shapes=[pltpu.VMEM((tm, d), BF16)],
        compiler_params=_params(("arbitrary", "arbitrary")),
        name="in_projection",
    )(x2d, mod_l.reshape(MOD_ROWS, 1, 3 * d), g_norm_l.reshape(1, d), w_in_l)


def _prep_kernel(*refs, use_rope, emit_k_f32):
    q_ref, k_ref, v_ref, gq_ref, gk_ref, gm_ref = refs[:6]
    refs = refs[6:]
    if use_rope:
        cos_ref, sa_ref, sb_ref = refs[:3]
        refs = refs[3:]
    qo_ref, ko_ref, vt_ref = refs[:3]
    kf_ref, vf_ref = refs[3:5] if emit_k_f32 else (None, None)

    def group_norm(x, g):
        sq = (x * x).astype(BF16)
        parts = [jnp.dot(sq[:, i * 256:(i + 1) * 256], gm_ref[...], preferred_element_type=F32)
                 for i in range(QK_W // 256)]
        ms = jnp.concatenate(parts, axis=1)
        return x * lax.rsqrt(ms + EPS) * g

    def rope(x):
        cos, sa, sb = cos_ref[...], sa_ref[...], sb_ref[...]
        outs = []
        for i in range(QK_W // HEAD_W):
            xc = x[:, i * HEAD_W:(i + 1) * HEAD_W]
            outs.append(xc * cos + pltpu.roll(xc, 16, 1) * sa + pltpu.roll(xc, HEAD_W - 16, 1) * sb)
        return jnp.concatenate(outs, axis=1)

    q = group_norm(q_ref[...].astype(F32), gq_ref[...])
    k = group_norm(k_ref[...].astype(F32), gk_ref[...])
    if use_rope:
        q, k = rope(q), rope(k)
    qo_ref[...] = (q * (QK_DIM ** -0.5 * math.log2(math.e))).astype(BF16)
    ko_ref[...] = k.astype(BF16)
    if emit_k_f32:
        kf_ref[...] = k
        vf_ref[...] = v_ref[...].astype(F32)
    for h in range(N_HEADS):
        vh = v_ref[:, h * HEAD_W:(h + 1) * HEAD_W].astype(F32)
        vt_ref[h * HEAD_W:(h + 1) * HEAD_W, :] = vh.T.astype(BF16)


def _qkv_prep(proj, g_q_l, g_k_l, rope_tabs, *, emit_k_f32):
    b, n, _ = proj.shape
    tm = min(512, n)
    use_rope = rope_tabs is not None
    reps = QK_W // QK_DIM
    in_specs = [pl.BlockSpec((None, tm, QK_W), lambda bi, i: (bi, i, COL_Q // QK_W)),
                pl.BlockSpec((None, tm, QK_W), lambda bi, i: (bi, i, COL_K // QK_W)),
                pl.BlockSpec((None, tm, V_W), lambda bi, i: (bi, i, COL_V // V_W)),
                pl.BlockSpec((1, QK_W), lambda bi, i: (0, 0)),
                pl.BlockSpec((1, QK_W), lambda bi, i: (0, 0)),
                pl.BlockSpec((256, 256), lambda bi, i: (0, 0))]
    args = [proj, proj, proj, jnp.tile(g_q_l, reps).reshape(1, QK_W), jnp.tile(g_k_l, reps).reshape(1, QK_W),
            _group_mean_matrix()]
    if use_rope:
        in_specs += [pl.BlockSpec((tm, HEAD_W), lambda bi, i: (i, 0))] * 3
        args += list(rope_tabs)
    out_specs = [pl.BlockSpec((None, tm, QK_W), lambda bi, i: (bi, i, 0)),
                 pl.BlockSpec((None, tm, QK_W), lambda bi, i: (bi, i, 0)),
                 pl.BlockSpec((None, V_W, tm), lambda bi, i: (bi, 0, i))]
    out_shape = [jax.ShapeDtypeStruct((b, n, QK_W), BF16),
                 jax.ShapeDtypeStruct((b, n, QK_W), BF16),
                 jax.ShapeDtypeStruct((b, V_W, n), BF16)]
    if emit_k_f32:
        out_specs += [pl.BlockSpec((None, tm, QK_W), lambda bi, i: (bi, i, 0)),
                      pl.BlockSpec((None, tm, V_W), lambda bi, i: (bi, i, 0))]
        out_shape += [jax.ShapeDtypeStruct((b, n, QK_W), F32), jax.ShapeDtypeStruct((b, n, V_W), F32)]
    return pl.pallas_call(
        functools.partial(_prep_kernel, use_rope=use_rope, emit_k_f32=emit_k_f32),
        grid=(b, n // tm),
        in_specs=in_specs, out_specs=out_specs, out_shape=out_shape,
        compiler_params=_params(("arbitrary", "arbitrary")),
        name="qkv_prep",
    )(*args)


ATTN_QUERY_TILE = 256
ATTN_TILES_PER_STEP = 8
ATTN_KEY_CHUNK = 1024
ATTN_HEADS_PER_STEP_SHORT = 4


def _attn_kernel(*refs, n_new, n_cache, k_chunk, lam_init, tq, n_tiles, heads):
    q_ref, kn_ref, vnt_ref = refs[:3]
    refs = refs[3:]
    if n_cache:
        kc_ref, vc_ref = refs[:2]
        refs = refs[2:]
    za_ref, gsub_ref, lamp_ref, o_ref = refs[:4]
    s_refs = refs[4:6]
    if n_cache:
        assert heads == 1
        kall_ref, vcts_ref = refs[6:8]

        @pl.when(pl.program_id(2) == 0)
        def _():
            kall_ref[0:n_cache, :] = kc_ref[...].astype(BF16)
            kall_ref[n_cache:n_cache + n_new, :] = kn_ref[...]
            vcts_ref[...] = vc_ref[...].T.astype(BF16)
    else:
        kall_ref = kn_ref

    def head_lanes(hl):
        return slice(hl * HEAD_W, (hl + 1) * HEAD_W)

    chunks = []
    if n_cache:
        chunks.append((0, n_cache, lambda hl: vcts_ref[...]))
    for j in range(n_new // k_chunk):
        chunks.append((n_cache + j * k_chunk, k_chunk,
                       lambda hl, j=j: vnt_ref[head_lanes(hl), j * k_chunk:(j + 1) * k_chunk]))

    lp = lamp_ref[...]
    lam = (jnp.exp(jnp.sum(lp[0:1] * lp[1:2], axis=1, keepdims=True))
           - jnp.exp(jnp.sum(lp[2:3] * lp[3:4], axis=1, keepdims=True)) + lam_init)

    def masked_q(hl, tile, c):
        q = q_ref[tile * tq:(tile + 1) * tq, head_lanes(hl)]
        lane = lax.broadcasted_iota(jnp.int32, q.shape, 1)
        keep = (lane < QK_DIM) if c == 0 else (lane >= QK_DIM)
        return jnp.where(keep, q, jnp.zeros_like(q))

    def scores(slot, hl, qc, chunk, m):
        off, rows = chunk[:2]
        s = lax.dot_general(kall_ref[off:off + rows, head_lanes(hl)], qc, (((1,), (1,)), ((), ())),
                            preferred_element_type=F32)
        s_refs[slot][off:off + rows, :] = s
        mc = jnp.max(s, axis=0, keepdims=True)
        return mc if m is None else jnp.maximum(m, mc)

    def weigh(slot, hl, chunk, m, state):
        off, rows, vt_get = chunk
        p = jnp.exp2(s_refs[slot][off:off + rows, :] - m)
        lc = jnp.sum(p, axis=0, keepdims=True)
        d = jnp.dot(vt_get(hl), p.astype(BF16), preferred_element_type=F32)
        return (lc, d) if state is None else (state[0] + lc, state[1] + d)

    def finish(hl, tile, state0, state1):
        o_t = state0[1] * (1.0 / state0[0]) - state1[1] * (lam / state1[0])
        ms = jnp.mean(o_t * o_t, axis=0, keepdims=True)
        o = (o_t * lax.rsqrt(ms + EPS)).T
        rows = slice(tile * tq, (tile + 1) * tq)
        za = za_ref[rows, head_lanes(hl)].astype(F32)
        o = o * (gsub_ref[...] * (1.0 - lam_init)) * _silu(za)
        o_ref[rows, head_lanes(hl)] = o.astype(o_ref.dtype)

    jobs = [(hl, t, c) for hl in range(heads) for t in range(n_tiles) for c in range(2)]
    maxes, accs = {}, {}
    for i in range(len(jobs) + 1):
        qc = masked_q(*jobs[i]) if i < len(jobs) else None
        m_new = acc = None
        for chunk in chunks:
            if i < len(jobs):
                m_new = scores(i % 2, jobs[i][0], qc, chunk, m_new)
            if i > 0:
                acc = weigh((i - 1) % 2, jobs[i - 1][0], chunk, maxes[i - 1], acc)
        maxes[i] = m_new
        if i > 0:
            hl, tile, c = jobs[i - 1]
            accs[c] = acc
            if c == 1:
                finish(hl, tile, accs[0], accs[1])


def _diff_attention(qn, kn, vnt, proj, g_sub_l, lam_params_l, cache_k4, cache_v4, layer_idx, lam_init):
    b, n, _ = qn.shape
    tq = min(ATTN_QUERY_TILE, n)
    n_tiles = min(ATTN_TILES_PER_STEP, n // tq)
    tb = tq * n_tiles
    k_chunk = min(ATTN_KEY_CHUNK, n)
    n_cache = 0 if cache_k4 is None else cache_k4.shape[2]
    heads = 1 if (n_cache or n // tb > 1) else ATTN_HEADS_PER_STEP_SHORT
    hw = heads * HEAD_W
    in_specs = [pl.BlockSpec((None, tb, hw), lambda bi, h, i: (bi, i, h)),
                pl.BlockSpec((None, n, hw), lambda bi, h, i: (bi, 0, h)),
                pl.BlockSpec((None, hw, n), lambda bi, h, i: (bi, h, 0))]
    args = [qn, kn, vnt]
    scratch = [pltpu.VMEM((n_cache + n, tq), F32), pltpu.VMEM((n_cache + n, tq), F32)]
    if n_cache:
        in_specs += [pl.BlockSpec((None, None, n_cache, HEAD_W), lambda bi, h, i: (bi, layer_idx, 0, h)),
                     pl.BlockSpec((None, None, n_cache, HEAD_W), lambda bi, h, i: (bi, layer_idx, 0, h))]
        args += [cache_k4, cache_v4]
        scratch += [pltpu.VMEM((n_cache + n, HEAD_W), BF16), pltpu.VMEM((HEAD_W, n_cache), BF16)]
    in_specs += [pl.BlockSpec((None, tb, hw), lambda bi, h, i: (bi, i, COL_ZA // hw + h)),
                 pl.BlockSpec((1, V_DIM), lambda bi, h, i: (0, 0)),
                 pl.BlockSpec((4, QK_DIM), lambda bi, h, i: (0, 0))]
    args += [proj, g_sub_l.reshape(1, V_DIM), lam_params_l]
    return pl.pallas_call(
        functools.partial(_attn_kernel, n_new=n, n_cache=n_cache, k_chunk=k_chunk, lam_init=lam_init, tq=tq,
                          n_tiles=n_tiles, heads=heads),
        grid=(b, N_HEADS // heads, n // tb),
        in_specs=in_specs,
        out_specs=pl.BlockSpec((None, tb, hw), lambda bi, h, i: (bi, i, h)),
        out_shape=jax.ShapeDtypeStruct((b, n, V_W), BF16),
        scratch_shapes=scratch,
        compiler_params=_params(("arbitrary", "arbitrary", "arbitrary")),
        name="diff_attention",
    )(*args)


def _fourier_long_kernel(u_ref, zf_ref, m1_ref, tw_ref, m2_ref, cd_ref, o_ref, x_ref, a_ref, z_ref, *, n):
    side, blk = 64, SUBLANES
    gc = F_GROUP_DIM
    x_ref[...] = u_ref[...].astype(F32).reshape(side, side, gc)
    for bb in range(side // blk):
        xb = x_ref[:, bb * blk:(bb + 1) * blk, :].reshape(side * blk, gc).astype(BF16)
        ab = jnp.dot(m1_ref[...], xb, preferred_element_type=F32)
        for bl in range(blk):
            b = bb * blk + bl
            ar = ab[bl * 2 * side:bl * 2 * side + side]
            ai = ab[bl * 2 * side + side:(bl + 1) * 2 * side]
            ct = jnp.concatenate([tw_ref[0, b * side:(b + 1) * side, :]] * (gc // LANES), axis=1)
            st = jnp.concatenate([tw_ref[1, b * side:(b + 1) * side, :]] * (gc // LANES), axis=1)
            a_ref[0, b] = ar * ct + ai * st
            a_ref[1, b] = ai * ct - ar * st
    for cb in range(side // blk):
        ac = jnp.concatenate([a_ref[ri, :, cb * blk:(cb + 1) * blk, :].reshape(side * blk, gc) for ri in range(2)],
                             axis=0).astype(BF16)
        zc = jnp.dot(m2_ref[...], ac, preferred_element_type=F32)
        for ri in range(2):
            z_ref[ri, :, cb * blk:(cb + 1) * blk, :] = zc[ri * side * blk:(ri + 1) * side * blk].reshape(side, blk, gc)
    d_rows = 8
    cd = cd_ref[...]
    for r in range(side // d_rows):
        sl = slice(r * d_rows * side, (r + 1) * d_rows * side)
        zr = z_ref[0, r * d_rows:(r + 1) * d_rows].reshape(d_rows * side, gc).astype(BF16)
        zi = z_ref[1, r * d_rows:(r + 1) * d_rows].reshape(d_rows * side, gc).astype(BF16)
        y = (jnp.dot(zr, cd[:gc], preferred_element_type=F32)
             + jnp.dot(zi, cd[gc:], preferred_element_type=F32))
        o_ref[sl, :] = (y * _silu(zf_ref[sl, :].astype(F32))).astype(o_ref.dtype)


def _fourier_short_kernel(u_ref, zf_ref, fn_ref, cd_ref, o_ref, *, n):
    cd = cd_ref[...]
    u = u_ref[...]
    fn = fn_ref[...]
    for g in range(F_GROUPS):
        cols = slice(g * F_GROUP_DIM, (g + 1) * F_GROUP_DIM)
        ug = u[:, cols]
        tc = jnp.dot(ug, cd[:F_GROUP_DIM], preferred_element_type=F32).astype(BF16)
        ts = jnp.dot(ug, cd[F_GROUP_DIM:], preferred_element_type=F32).astype(BF16)
        y = (jnp.dot(fn[:, :n], tc, preferred_element_type=F32)
             + jnp.dot(fn[:, n:], ts, preferred_element_type=F32))
        o_ref[:, cols] = (y * _silu(zf_ref[:, cols].astype(F32))).astype(o_ref.dtype)


def _fourier_branch(proj):
    b, n, _ = proj.shape
    gc = F_GROUP_DIM
    scale = 1.0 / math.sqrt(n * gc)
    cd = _channel_dft(scale)
    u_spec = pl.BlockSpec((None, n, gc), lambda bi, g: (bi, 0, COL_UF // gc + g))
    zf_spec = pl.BlockSpec((None, n, gc), lambda bi, g: (bi, 0, COL_ZF // gc + g))
    cd_spec = pl.BlockSpec((2 * gc, gc), lambda bi, g: (0, 0))
    out_spec = pl.BlockSpec((None, n, gc), lambda bi, g: (bi, 0, g))
    out_shape = jax.ShapeDtypeStruct((b, n, F_DIM), BF16)
    if n == 64 * 64:
        m1, tw, m2 = _long_dft_tables(n)
        const = lambda a: pl.BlockSpec(a.shape, lambda bi, g: (0,) * a.ndim, pipeline_mode=pl.Buffered(1))
        return pl.pallas_call(
            functools.partial(_fourier_long_kernel, n=n),
            grid=(b, F_GROUPS),
            in_specs=[u_spec, zf_spec, const(m1), const(tw), const(m2), cd_spec],
            out_specs=out_spec, out_shape=out_shape,
            scratch_shapes=[pltpu.VMEM((64, 64, gc), F32),
                            pltpu.VMEM((2, 64, 64, gc), F32),
                            pltpu.VMEM((2, 64, 64, gc), F32)],
            compiler_params=_params(("arbitrary", "arbitrary")),
            name="fourier_long",
        )(proj, proj, m1, tw, m2, cd)
    fn = _short_dft_table(n)
    return pl.pallas_call(
        functools.partial(_fourier_short_kernel, n=n),
        grid=(b,),
        in_specs=[pl.BlockSpec((None, n, F_DIM), lambda bi: (bi, 0, COL_UF // F_DIM)),
                  pl.BlockSpec((None, n, F_DIM), lambda bi: (bi, 0, COL_ZF // F_DIM)),
                  pl.BlockSpec((n, 2 * n), lambda bi: (0, 0)),
                  pl.BlockSpec((2 * gc, gc), lambda bi: (0, 0))],
        out_specs=pl.BlockSpec((None, n, F_DIM), lambda bi: (bi, 0, 0)), out_shape=out_shape,
        compiler_params=_params(("arbitrary",)),
        name="fourier_short",
    )(proj, proj, fn, cd)


def _outproj_kernel(yf_ref, ya_ref, gf_ref, ga_ref, x_ref, mod_ref, wf_ref, wa_ref, wo_ref, o_ref, *, d):
    f = jnp.dot(yf_ref[...], wf_ref[...], preferred_element_type=F32)
    a = jnp.dot(ya_ref[...], wa_ref[...], preferred_element_type=F32)
    merged = jax.nn.sigmoid(gf_ref[...].astype(F32)) * f + jax.nn.sigmoid(ga_ref[...].astype(F32)) * a
    out = jnp.dot(merged.astype(BF16), wo_ref[...], preferred_element_type=F32)
    gate = mod_ref[:, 2 * d:3 * d]
    o_ref[...] = x_ref[...] + gate * out


def _out_projection(yf2d, ya2d, proj2d, x2d, mod_l, wf, wa, wo, *, rows_per_batch, mod_row0, per_batch_mod):
    m, d = x2d.shape
    tm = min(256, rows_per_batch)
    blocks_per_batch = rows_per_batch // tm
    if per_batch_mod:
        mod_map = lambda i: (mod_row0 + i // blocks_per_batch, 0, 0)
    else:
        mod_map = lambda i: (mod_row0, 0, 0)
    const = lambda shape: pl.BlockSpec(shape, lambda i: (0, 0), pipeline_mode=pl.Buffered(1))
    return pl.pallas_call(
        functools.partial(_outproj_kernel, d=d),
        grid=(m // tm,),
        in_specs=[pl.BlockSpec((tm, F_DIM), lambda i: (i, 0)),
                  pl.BlockSpec((tm, V_W), lambda i: (i, 0)),
                  pl.BlockSpec((tm, d), lambda i: (i, COL_GF // d)),
                  pl.BlockSpec((tm, d), lambda i: (i, COL_GF // d + 1)),
                  pl.BlockSpec((tm, d), lambda i: (i, 0)),
                  pl.BlockSpec((None, 1, 3 * d), mod_map),
                  const((F_DIM, d)), const((V_W, d)), const((d, d))],
        out_specs=pl.BlockSpec((tm, d), lambda i: (i, 0)),
        out_shape=jax.ShapeDtypeStruct((m, d), F32),
        compiler_params=_params(("arbitrary",)),
        name="out_projection",
    )(yf2d, ya2d, proj2d, proj2d, x2d, mod_l.reshape(MOD_ROWS, 1, 3 * d), wf, wa, wo)


def _layer(x, mod_l, weights_l, lam_init, *, mod_row0, per_batch_mod, rope_tabs, cache_k4, cache_v4, layer_idx,
           emit_cache):
    w_in, w_fproj, w_aproj, w_out, g_norm, g_q, g_k, g_sub, lam_params = weights_l
    b, n, d = x.shape
    x2d = x.reshape(b * n, d)
    proj2d = _in_projection(x2d, mod_l, g_norm, w_in, rows_per_batch=n, mod_row0=mod_row0,
                            per_batch_mod=per_batch_mod)
    proj = proj2d.reshape(b, n, -1)
    prep = _qkv_prep(proj, g_q, g_k, rope_tabs, emit_k_f32=emit_cache)
    qn, kn, vnt = prep[:3]
    ya = _diff_attention(qn, kn, vnt, proj, g_sub, lam_params, cache_k4, cache_v4, layer_idx, lam_init)
    yf = _fourier_branch(proj)
    y2d = _out_projection(yf.reshape(b * n, F_DIM), ya.reshape(b * n, V_W), proj2d, x2d, mod_l,
                          w_fproj, w_aproj, w_out, rows_per_batch=n, mod_row0=mod_row0,
                          per_batch_mod=per_batch_mod)
    y = y2d.reshape(b, n, d)
    if emit_cache:
        k_new = prep[3].reshape(b, n, N_HEADS, 2, QK_DIM)
        v_new = prep[4].reshape(b, n, N_HEADS, V_DIM)
        return y, k_new, v_new
    return y, None, None


def kernel(x_prompt, x_sample, c, cache_k, cache_v, c_ctx, w_in, w_fproj, w_aproj, w_out, w_mod, b_mod, g_norm,
           g_q, g_k, g_sub, lam_q1, lam_k1, lam_q2, lam_k2):
    depth = w_in.shape[0]
    d = x_prompt.shape[-1]
    dec_b, dec_n = x_sample.shape[0], x_sample.shape[1]
    assert 1 + dec_b <= MOD_ROWS

    cvecs = jnp.concatenate([c_ctx[None, :], c, jnp.zeros((MOD_ROWS - 1 - dec_b, d), F32)], axis=0)
    mod = _modulation(cvecs, w_mod, b_mod)

    w_in_b, w_fproj_b, w_aproj_b, w_out_b = (w.astype(BF16) for w in (w_in, w_fproj, w_aproj, w_out))
    lam_params = jnp.stack([lam_q1, lam_k1, lam_q2, lam_k2], axis=1)
    rope_tabs = _rope_tables(dec_n)
    past = cache_k.shape[2]
    cache_k4 = cache_k.reshape(dec_b, depth, past, QK_W)
    cache_v4 = cache_v.reshape(dec_b, depth, past, V_W)

    y_p, y_s = x_prompt, x_sample
    ks_out, vs_out = [], []
    for l in range(depth):
        lam_init = 0.8 - 0.6 * math.exp(-0.3 * l)
        weights_l = (w_in_b[l], w_fproj_b[l], w_aproj_b[l], w_out_b[l], g_norm[l], g_q[l], g_k[l], g_sub[l],
                     lam_params[l])
        y_p, k_l, v_l = _layer(y_p, mod[l], weights_l, lam_init, mod_row0=0, per_batch_mod=False, rope_tabs=None,
                               cache_k4=None, cache_v4=None, layer_idx=l, emit_cache=True)
        ks_out.append(k_l)
        vs_out.append(v_l)
        y_s, _, _ = _layer(y_s, mod[l], weights_l, lam_init, mod_row0=1, per_batch_mod=True, rope_tabs=rope_tabs,
                           cache_k4=cache_k4, cache_v4=cache_v4, layer_idx=l, emit_cache=False)
    return (y_p, y_s, jnp.stack(ks_out, axis=1), jnp.stack(vs_out, axis=1))
```

```python
import functools
import math

import numpy as np
import jax
import jax.numpy as jnp
from jax import lax
from jax.experimental import pallas as pl
from jax.experimental.pallas import tpu as pltpu

F32 = jnp.float32
BF16 = jnp.bfloat16

GRID_W = 64
F_GROUPS = 4
F_GROUP_DIM = 256
F_DIM = F_GROUPS * F_GROUP_DIM
N_HEADS = 8
QK_DIM = 64
V_DIM = 2 * QK_DIM
QK_W = N_HEADS * 2 * QK_DIM
V_W = N_HEADS * V_DIM
HEAD_W = 128
LANES = 128
SUBLANES = 8
ROPE_BASE = 10000.0
EPS = 1e-6

COL_UF, COL_ZF, COL_Q, COL_K, COL_V, COL_ZA, COL_GF = 0, 1024, 2048, 3072, 4096, 5120, 6144

MOD_ROWS = 8
VMEM_LIMIT = 56 * 1024 * 1024


def _params(sem, vmem=VMEM_LIMIT):
    return pltpu.CompilerParams(dimension_semantics=sem, vmem_limit_bytes=vmem)


def _silu(x):
    return x * jax.nn.sigmoid(x)


def _rope_tables(n_tokens):
    rows = n_tokens // GRID_W
    r = np.repeat(np.arange(rows, dtype=np.float32), GRID_W)
    c = np.tile(np.arange(GRID_W, dtype=np.float32), rows)
    half = QK_DIM // 2
    inv = (1.0 / (np.float32(ROPE_BASE) ** (np.arange(half // 2, dtype=np.float32) * np.float32(2.0) / np.float32(half)))).astype(np.float32)
    ang_r = (r[:, None] * inv[None, :]).astype(np.float32).astype(np.float64)
    ang_c = (c[:, None] * inv[None, :]).astype(np.float32).astype(np.float64)
    cos = np.zeros((n_tokens, QK_DIM)); sa = np.zeros((n_tokens, QK_DIM)); sb = np.zeros((n_tokens, QK_DIM))
    q = half // 2
    for sec, ang in ((0, ang_r), (1, ang_c)):
        base = sec * half
        cos[:, base:base + q] = np.cos(ang); cos[:, base + q:base + half] = np.cos(ang)
        sb[:, base:base + q] = -np.sin(ang)
        sa[:, base + q:base + half] = np.sin(ang)
    tile = lambda t: jnp.asarray(np.tile(t, (1, HEAD_W // QK_DIM)), F32)
    return tile(cos), tile(sa), tile(sb)


def _group_mean_matrix():
    m = np.kron(np.eye(256 // QK_DIM), np.full((QK_DIM, QK_DIM), 1.0 / QK_DIM))
    return jnp.asarray(m, BF16)


def _channel_dft(scale):
    c = np.arange(F_GROUP_DIM)
    ang = 2.0 * np.pi * np.outer(c, c) / F_GROUP_DIM
    return jnp.asarray(np.concatenate([np.cos(ang), np.sin(ang)], axis=0) * scale, BF16)


def _long_dft_tables(n):
    side, blk = 64, SUBLANES
    a = np.arange(side)
    ang = 2.0 * np.pi * np.outer(a, a) / side
    cs, sn = np.cos(ang), np.sin(ang)
    eye = np.eye(blk)
    f64 = np.stack([cs, -sn])
    m1 = np.einsum('rca,lm->lrcam', f64, eye).reshape(blk * 2 * side, side * blk)
    angt = 2.0 * np.pi * np.outer(a, a).reshape(-1) / n
    tw = np.stack([np.cos(angt), np.sin(angt)])[:, :, None] * np.ones((1, 1, LANES))
    w2 = np.stack([np.stack([cs, sn]), np.stack([-sn, cs])])
    m2 = np.einsum('rsdb,lm->rdlsbm', w2, eye).reshape(2 * side * blk, 2 * side * blk)
    return jnp.asarray(m1, BF16), jnp.asarray(tw, F32), jnp.asarray(m2, BF16)


def _short_dft_table(n):
    t = np.arange(n)
    ang = 2.0 * np.pi * np.outer(t, t) / n
    return jnp.asarray(np.concatenate([np.cos(ang), -np.sin(ang)], axis=1), BF16)


def _mod_kernel(c_ref, w_ref, b_ref, o_ref):
    s = _silu(c_ref[...])
    o_ref[...] = jnp.dot(s.astype(BF16), w_ref[...].astype(BF16), preferred_element_type=F32) + b_ref[...]


def _modulation(cvecs, w_mod, b_mod):
    depth, d, d3 = w_mod.shape
    tn = 1024 if d3 % 1024 == 0 else d3
    return pl.pallas_call(
        _mod_kernel,
        grid=(depth, d3 // tn),
        in_specs=[pl.BlockSpec((MOD_ROWS, d), lambda l, j: (0, 0)),
                  pl.BlockSpec((None, d, tn), lambda l, j: (l, 0, j)),
                  pl.BlockSpec((None, 1, tn), lambda l, j: (l, 0, j))],
        out_specs=pl.BlockSpec((None, MOD_ROWS, tn), lambda l, j: (l, 0, j)),
        out_shape=jax.ShapeDtypeStruct((depth, MOD_ROWS, d3), F32),
        compiler_params=_params(("arbitrary", "arbitrary")),
        name="modulation",
    )(cvecs, w_mod, b_mod.reshape(depth, 1, d3))


def _inproj_kernel(x_ref, mod_ref, g_ref, w_ref, o_ref, h_ref, *, d):
    @pl.when(pl.program_id(1) == 0)
    def _():
        x = x_ref[...]
        ms = jnp.mean(x * x, axis=-1, keepdims=True)
        shift = mod_ref[:, 0:d]
        scale = mod_ref[:, d:2 * d]
        h = x * lax.rsqrt(ms + EPS) * (g_ref[...] * (1.0 + scale)) + shift
        h_ref[...] = h.astype(BF16)

    o_ref[...] = jnp.dot(h_ref[...], w_ref[...], preferred_element_type=F32).astype(o_ref.dtype)


def _in_projection(x2d, mod_l, g_norm_l, w_in_l, *, rows_per_batch, mod_row0, per_batch_mod):
    m, d = x2d.shape
    n_cols = w_in_l.shape[1]
    tm = min(1024, rows_per_batch if per_batch_mod else m)
    tn = next(t for t in (2048, 1024, 512) if n_cols % t == 0)
    blocks_per_batch = rows_per_batch // tm
    if per_batch_mod:
        mod_map = lambda i, j: (mod_row0 + i // blocks_per_batch, 0, 0)
    else:
        mod_map = lambda i, j: (mod_row0, 0, 0)
    return pl.pallas_call(
        functools.partial(_inproj_kernel, d=d),
        grid=(m // tm, n_cols // tn),
        in_specs=[pl.BlockSpec((tm, d), lambda i, j: (i, 0)),
                  pl.BlockSpec((None, 1, 3 * d), mod_map),
                  pl.BlockSpec((1, d), lambda i, j: (0, 0)),
                  pl.BlockSpec((d, tn), lambda i, j: (0, j))],
        out_specs=pl.BlockSpec((tm, tn), lambda i, j: (i, j)),
        out_shape=jax.ShapeDtypeStruct((m, n_cols), BF16),
        scratch_shapes=[pltpu.VMEM((tm, d), BF16)],
        compiler_params=_params(("arbitrary", "arbitrary")),
        name="in_projection",
    )(x2d, mod_l.reshape(MOD_ROWS, 1, 3 * d), g_norm_l.reshape(1, d), w_in_l)


def _prep_kernel(*refs, use_rope, emit_k_f32):
    q_ref, k_ref, v_ref, gq_ref, gk_ref, gm_ref = refs[:6]
    refs = refs[6:]
    if use_rope:
        cos_ref, sa_ref, sb_ref = refs[:3]
        refs = refs[3:]
    qo_ref, ko_ref, vt_ref = refs[:3]
    kf_ref, vf_ref = refs[3:5] if emit_k_f32 else (None, None)

    def group_norm(x, g):
        sq = (x * x).astype(BF16)
        parts = [jnp.dot(sq[:, i * 256:(i + 1) * 256], gm_ref[...], preferred_element_type=F32)
                 for i in range(QK_W // 256)]
        ms = jnp.concatenate(parts, axis=1)
        return x * lax.rsqrt(ms + EPS) * g

    def rope(x):
        cos, sa, sb = cos_ref[...], sa_ref[...], sb_ref[...]
        outs = []
        for i in range(QK_W // HEAD_W):
            xc = x[:, i * HEAD_W:(i + 1) * HEAD_W]
            outs.append(xc * cos + pltpu.roll(xc, 16, 1) * sa + pltpu.roll(xc, HEAD_W - 16, 1) * sb)
        return jnp.concatenate(outs, axis=1)

    q = group_norm(q_ref[...].astype(F32), gq_ref[...])
    k = group_norm(k_ref[...].astype(F32), gk_ref[...])
    if use_rope:
        q, k = rope(q), rope(k)
    qo_ref[...] = (q * (QK_DIM ** -0.5 * math.log2(math.e))).astype(BF16)
    ko_ref[...] = k.astype(BF16)
    if emit_k_f32:
        kf_ref[...] = k
        vf_ref[...] = v_ref[...].astype(F32)
    for h in range(N_HEADS):
        vh = v_ref[:, h * HEAD_W:(h + 1) * HEAD_W].astype(F32)
        vt_ref[h * HEAD_W:(h + 1) * HEAD_W, :] = vh.T.astype(BF16)


def _qkv_prep(proj, g_q_l, g_k_l, rope_tabs, *, emit_k_f32):
    b, n, _ = proj.shape
    tm = min(512, n)
    use_rope = rope_tabs is not None
    reps = QK_W // QK_DIM
    in_specs = [pl.BlockSpec((None, tm, QK_W), lambda bi, i: (bi, i, COL_Q // QK_W)),
                pl.BlockSpec((None, tm, QK_W), lambda bi, i: (bi, i, COL_K // QK_W)),
                pl.BlockSpec((None, tm, V_W), lambda bi, i: (bi, i, COL_V // V_W)),
                pl.BlockSpec((1, QK_W), lambda bi, i: (0, 0)),
                pl.BlockSpec((1, QK_W), lambda bi, i: (0, 0)),
                pl.BlockSpec((256, 256), lambda bi, i: (0, 0))]
    args = [proj, proj, proj, jnp.tile(g_q_l, reps).reshape(1, QK_W), jnp.tile(g_k_l, reps).reshape(1, QK_W),
            _group_mean_matrix()]
    if use_rope:
        in_specs += [pl.BlockSpec((tm, HEAD_W), lambda bi, i: (i, 0))] * 3
        args += list(rope_tabs)
    out_specs = [pl.BlockSpec((None, tm, QK_W), lambda bi, i: (bi, i, 0)),
                 pl.BlockSpec((None, tm, QK_W), lambda bi, i: (bi, i, 0)),
                 pl.BlockSpec((None, V_W, tm), lambda bi, i: (bi, 0, i))]
    out_shape = [jax.ShapeDtypeStruct((b, n, QK_W), BF16),
                 jax.ShapeDtypeStruct((b, n, QK_W), BF16),
                 jax.ShapeDtypeStruct((b, V_W, n), BF16)]
    if emit_k_f32:
        out_specs += [pl.BlockSpec((None, tm, QK_W), lambda bi, i: (bi, i, 0)),
                      pl.BlockSpec((None, tm, V_W), lambda bi, i: (bi, i, 0))]
        out_shape += [jax.ShapeDtypeStruct((b, n, QK_W), F32), jax.ShapeDtypeStruct((b, n, V_W), F32)]
    return pl.pallas_call(
        functools.partial(_prep_kernel, use_rope=use_rope, emit_k_f32=emit_k_f32),
        grid=(b, n // tm),
        in_specs=in_specs, out_specs=out_specs, out_shape=out_shape,
        compiler_params=_params(("arbitrary", "arbitrary")),
        name="qkv_prep",
    )(*args)


ATTN_QUERY_TILE = 256
ATTN_TILES_PER_STEP = 8
ATTN_KEY_CHUNK = 1024
ATTN_HEADS_PER_STEP_SHORT = 8


def _attn_kernel(*refs, n_new, n_cache, k_chunk, lam_init, tq, n_tiles, heads):
    q_ref, kn_ref, vnt_ref = refs[:3]
    refs = refs[3:]
    if n_cache:
        kc_ref, vc_ref = refs[:2]
        refs = refs[2:]
    za_ref, gsub_ref, lamp_ref, o_ref = refs[:4]
    s_refs = refs[4:6]
    if n_cache:
        assert heads == 1
        kall_ref, vcts_ref = refs[6:8]

        @pl.when(pl.program_id(2) == 0)
        def _():
            kall_ref[0:n_cache, :] = kc_ref[...].astype(BF16)
            kall_ref[n_cache:n_cache + n_new, :] = kn_ref[...]
            vcts_ref[...] = vc_ref[...].T.astype(BF16)
    else:
        kall_ref = kn_ref

    def head_lanes(hl):
        return slice(hl * HEAD_W, (hl + 1) * HEAD_W)

    chunks = []
    if n_cache:
        chunks.append((0, n_cache, lambda hl: vcts_ref[...]))
    for j in range(n_new // k_chunk):
        chunks.append((n_cache + j * k_chunk, k_chunk,
                       lambda hl, j=j: vnt_ref[head_lanes(hl), j * k_chunk:(j + 1) * k_chunk]))

    lp = lamp_ref[...]
    lam = (jnp.exp(jnp.sum(lp[0:1] * lp[1:2], axis=1, keepdims=True))
           - jnp.exp(jnp.sum(lp[2:3] * lp[3:4], axis=1, keepdims=True)) + lam_init)

    def masked_q(hl, tile, c):
        q = q_ref[tile * tq:(tile + 1) * tq, head_lanes(hl)]
        lane = lax.broadcasted_iota(jnp.int32, q.shape, 1)
        keep = (lane < QK_DIM) if c == 0 else (lane >= QK_DIM)
        return jnp.where(keep, q, jnp.zeros_like(q))

    def scores(slot, hl, qc, chunk, m):
        off, rows = chunk[:2]
        s = lax.dot_general(kall_ref[off:off + rows, head_lanes(hl)], qc, (((1,), (1,)), ((), ())),
                            preferred_element_type=F32)
        s_refs[slot][off:off + rows, :] = s
        mc = jnp.max(s, axis=0, keepdims=True)
        return mc if m is None else jnp.maximum(m, mc)

    def weigh(slot, hl, chunk, m, state):
        off, rows, vt_get = chunk
        p = jnp.exp2(s_refs[slot][off:off + rows, :] - m)
        lc = jnp.sum(p, axis=0, keepdims=True)
        d = jnp.dot(vt_get(hl), p.astype(BF16), preferred_element_type=F32)
        return (lc, d) if state is None else (state[0] + lc, state[1] + d)

    def finish(hl, tile, state0, state1):
        o_t = state0[1] * (1.0 / state0[0]) - state1[1] * (lam / state1[0])
        ms = jnp.mean(o_t * o_t, axis=0, keepdims=True)
        o = (o_t * lax.rsqrt(ms + EPS)).T
        rows = slice(tile * tq, (tile + 1) * tq)
        za = za_ref[rows, head_lanes(hl)].astype(F32)
        o = o * (gsub_ref[...] * (1.0 - lam_init)) * _silu(za)
        o_ref[rows, head_lanes(hl)] = o.astype(o_ref.dtype)

    jobs = [(hl, t, c) for hl in range(heads) for t in range(n_tiles) for c in range(2)]
    maxes, accs = {}, {}
    for i in range(len(jobs) + 1):
        qc = masked_q(*jobs[i]) if i < len(jobs) else None
        m_new = acc = None
        for chunk in chunks:
            if i < len(jobs):
                m_new = scores(i % 2, jobs[i][0], qc, chunk, m_new)
            if i > 0:
                acc = weigh((i - 1) % 2, jobs[i - 1][0], chunk, maxes[i - 1], acc)
        maxes[i] = m_new
        if i > 0:
            hl, tile, c = jobs[i - 1]
            accs[c] = acc
            if c == 1:
                finish(hl, tile, accs[0], accs[1])


def _diff_attention(qn, kn, vnt, proj, g_sub_l, lam_params_l, cache_k4, cache_v4, layer_idx, lam_init):
    b, n, _ = qn.shape
    tq = min(ATTN_QUERY_TILE, n)
    n_tiles = min(ATTN_TILES_PER_STEP, n // tq)
    tb = tq * n_tiles
    k_chunk = min(ATTN_KEY_CHUNK, n)
    n_cache = 0 if cache_k4 is None else cache_k4.shape[2]
    heads = 1 if (n_cache or n // tb > 1) else ATTN_HEADS_PER_STEP_SHORT
    hw = heads * HEAD_W
    in_specs = [pl.BlockSpec((None, tb, hw), lambda bi, h, i: (bi, i, h)),
                pl.BlockSpec((None, n, hw), lambda bi, h, i: (bi, 0, h)),
                pl.BlockSpec((None, hw, n), lambda bi, h, i: (bi, h, 0))]
    args = [qn, kn, vnt]
    scratch = [pltpu.VMEM((n_cache + n, tq), F32), pltpu.VMEM((n_cache + n, tq), F32)]
    if n_cache:
        in_specs += [pl.BlockSpec((None, None, n_cache, HEAD_W), lambda bi, h, i: (bi, layer_idx, 0, h)),
                     pl.BlockSpec((None, None, n_cache, HEAD_W), lambda bi, h, i: (bi, layer_idx, 0, h))]
        args += [cache_k4, cache_v4]
        scratch += [pltpu.VMEM((n_cache + n, HEAD_W), BF16), pltpu.VMEM((HEAD_W, n_cache), BF16)]
    in_specs += [pl.BlockSpec((None, tb, hw), lambda bi, h, i: (bi, i, COL_ZA // hw + h)),
                 pl.BlockSpec((1, V_DIM), lambda bi, h, i: (0, 0)),
                 pl.BlockSpec((4, QK_DIM), lambda bi, h, i: (0, 0))]
    args += [proj, g_sub_l.reshape(1, V_DIM), lam_params_l]
    return pl.pallas_call(
        functools.partial(_attn_kernel, n_new=n, n_cache=n_cache, k_chunk=k_chunk, lam_init=lam_init, tq=tq,
                          n_tiles=n_tiles, heads=heads),
        grid=(b, N_HEADS // heads, n // tb),
        in_specs=in_specs,
        out_specs=pl.BlockSpec((None, tb, hw), lambda bi, h, i: (bi, i, h)),
        out_shape=jax.ShapeDtypeStruct((b, n, V_W), BF16),
        scratch_shapes=scratch,
        compiler_params=_params(("arbitrary", "arbitrary", "arbitrary")),
        name="diff_attention",
    )(*args)


def _fourier_long_kernel(u_ref, zf_ref, m1_ref, tw_ref, m2_ref, cd_ref, o_ref, x_ref, a_ref, z_ref, *, n):
    side, blk = 64, SUBLANES
    gc = F_GROUP_DIM
    x_ref[...] = u_ref[...].astype(F32).reshape(side, side, gc)
    for bb in range(side // blk):
        xb = x_ref[:, bb * blk:(bb + 1) * blk, :].reshape(side * blk, gc).astype(BF16)
        ab = jnp.dot(m1_ref[...], xb, preferred_element_type=F32)
        for bl in range(blk):
            b = bb * blk + bl
            ar = ab[bl * 2 * side:bl * 2 * side + side]
            ai = ab[bl * 2 * side + side:(bl + 1) * 2 * side]
            ct = jnp.concatenate([tw_ref[0, b * side:(b + 1) * side, :]] * (gc // LANES), axis=1)
            st = jnp.concatenate([tw_ref[1, b * side:(b + 1) * side, :]] * (gc // LANES), axis=1)
            a_ref[0, b] = ar * ct + ai * st
            a_ref[1, b] = ai * ct - ar * st
    for cb in range(side // blk):
        ac = jnp.concatenate([a_ref[ri, :, cb * blk:(cb + 1) * blk, :].reshape(side * blk, gc) for ri in range(2)],
                             axis=0).astype(BF16)
        zc = jnp.dot(m2_ref[...], ac, preferred_element_type=F32)
        for ri in range(2):
            z_ref[ri, :, cb * blk:(cb + 1) * blk, :] = zc[ri * side * blk:(ri + 1) * side * blk].reshape(side, blk, gc)
    d_rows = 8
    cd = cd_ref[...]
    for r in range(side // d_rows):
        sl = slice(r * d_rows * side, (r + 1) * d_rows * side)
        zr = z_ref[0, r * d_rows:(r + 1) * d_rows].reshape(d_rows * side, gc).astype(BF16)
        zi = z_ref[1, r * d_rows:(r + 1) * d_rows].reshape(d_rows * side, gc).astype(BF16)
        y = (jnp.dot(zr, cd[:gc], preferred_element_type=F32)
             + jnp.dot(zi, cd[gc:], preferred_element_type=F32))
        o_ref[sl, :] = (y * _silu(zf_ref[sl, :].astype(F32))).astype(o_ref.dtype)


def _fourier_short_kernel(u_ref, zf_ref, fn_ref, cd_ref, o_ref, *, n):
    cd = cd_ref[...]
    u = u_ref[...]
    fn = fn_ref[...]
    for g in range(F_GROUPS):
        cols = slice(g * F_GROUP_DIM, (g + 1) * F_GROUP_DIM)
        ug = u[:, cols]
        tc = jnp.dot(ug, cd[:F_GROUP_DIM], preferred_element_type=F32).astype(BF16)
        ts = jnp.dot(ug, cd[F_GROUP_DIM:], preferred_element_type=F32).astype(BF16)
        y = (jnp.dot(fn[:, :n], tc, preferred_element_type=F32)
             + jnp.dot(fn[:, n:], ts, preferred_element_type=F32))
        o_ref[:, cols] = (y * _silu(zf_ref[:, cols].astype(F32))).astype(o_ref.dtype)


def _fourier_branch(proj):
    b, n, _ = proj.shape
    gc = F_GROUP_DIM
    scale = 1.0 / math.sqrt(n * gc)
    cd = _channel_dft(scale)
    u_spec = pl.BlockSpec((None, n, gc), lambda bi, g: (bi, 0, COL_UF // gc + g))
    zf_spec = pl.BlockSpec((None, n, gc), lambda bi, g: (bi, 0, COL_ZF // gc + g))
    cd_spec = pl.BlockSpec((2 * gc, gc), lambda bi, g: (0, 0))
    out_spec = pl.BlockSpec((None, n, gc), lambda bi, g: (bi, 0, g))
    out_shape = jax.ShapeDtypeStruct((b, n, F_DIM), BF16)
    if n == 64 * 64:
        m1, tw, m2 = _long_dft_tables(n)
        const = lambda a: pl.BlockSpec(a.shape, lambda bi, g: (0,) * a.ndim, pipeline_mode=pl.Buffered(1))
        return pl.pallas_call(
            functools.partial(_fourier_long_kernel, n=n),
            grid=(b, F_GROUPS),
            in_specs=[u_spec, zf_spec, const(m1), const(tw), const(m2), cd_spec],
            out_specs=out_spec, out_shape=out_shape,
            scratch_shapes=[pltpu.VMEM((64, 64, gc), F32),
                            pltpu.VMEM((2, 64, 64, gc), F32),
                            pltpu.VMEM((2, 64, 64, gc), F32)],
            compiler_params=_params(("arbitrary", "arbitrary")),
            name="fourier_long",
        )(proj, proj, m1, tw, m2, cd)
    fn = _short_dft_table(n)
    return pl.pallas_call(
        functools.partial(_fourier_short_kernel, n=n),
        grid=(b,),
        in_specs=[pl.BlockSpec((None, n, F_DIM), lambda bi: (bi, 0, COL_UF // F_DIM)),
                  pl.BlockSpec((None, n, F_DIM), lambda bi: (bi, 0, COL_ZF // F_DIM)),
                  pl.BlockSpec((n, 2 * n), lambda bi: (0, 0)),
                  pl.BlockSpec((2 * gc, gc), lambda bi: (0, 0))],
        out_specs=pl.BlockSpec((None, n, F_DIM), lambda bi: (bi, 0, 0)), out_shape=out_shape,
        compiler_params=_params(("arbitrary",)),
        name="fourier_short",
    )(proj, proj, fn, cd)


def _outproj_kernel(yf_ref, ya_ref, gf_ref, ga_ref, x_ref, mod_ref, wf_ref, wa_ref, wo_ref, o_ref, *, d):
    f = jnp.dot(yf_ref[...], wf_ref[...], preferred_element_type=F32)
    a = jnp.dot(ya_ref[...], wa_ref[...], preferred_element_type=F32)
    merged = jax.nn.sigmoid(gf_ref[...].astype(F32)) * f + jax.nn.sigmoid(ga_ref[...].astype(F32)) * a
    out = jnp.dot(merged.astype(BF16), wo_ref[...], preferred_element_type=F32)
    gate = mod_ref[:, 2 * d:3 * d]
    o_ref[...] = x_ref[...] + gate * out


def _out_projection(yf2d, ya2d, proj2d, x2d, mod_l, wf, wa, wo, *, rows_per_batch, mod_row0, per_batch_mod):
    m, d = x2d.shape
    tm = min(512, rows_per_batch if per_batch_mod else m)
    blocks_per_batch = rows_per_batch // tm
    if per_batch_mod:
        mod_map = lambda i: (mod_row0 + i // blocks_per_batch, 0, 0)
    else:
        mod_map = lambda i: (mod_row0, 0, 0)
    const = lambda shape: pl.BlockSpec(shape, lambda i: (0, 0), pipeline_mode=pl.Buffered(1))
    return pl.pallas_call(
        functools.partial(_outproj_kernel, d=d),
        grid=(m // tm,),
        in_specs=[pl.BlockSpec((tm, F_DIM), lambda i: (i, 0)),
                  pl.BlockSpec((tm, V_W), lambda i: (i, 0)),
                  pl.BlockSpec((tm, d), lambda i: (i, COL_GF // d)),
                  pl.BlockSpec((tm, d), lambda i: (i, COL_GF // d + 1)),
                  pl.BlockSpec((tm, d), lambda i: (i, 0)),
                  pl.BlockSpec((None, 1, 3 * d), mod_map),
                  const((F_DIM, d)), const((V_W, d)), const((d, d))],
        out_specs=pl.BlockSpec((tm, d), lambda i: (i, 0)),
        out_shape=jax.ShapeDtypeStruct((m, d), F32),
        compiler_params=_params(("arbitrary",)),
        name="out_projection",
    )(yf2d, ya2d, proj2d, proj2d, x2d, mod_l.reshape(MOD_ROWS, 1, 3 * d), wf, wa, wo)


def _layer(x, mod_l, weights_l, lam_init, *, mod_row0, per_batch_mod, rope_tabs, cache_k4, cache_v4, layer_idx,
           emit_cache):
    w_in, w_fproj, w_aproj, w_out, g_norm, g_q, g_k, g_sub, lam_params = weights_l
    b, n, d = x.shape
    x2d = x.reshape(b * n, d)
    proj2d = _in_projection(x2d, mod_l, g_norm, w_in, rows_per_batch=n, mod_row0=mod_row0,
                            per_batch_mod=per_batch_mod)
    proj = proj2d.reshape(b, n, -1)
    prep = _qkv_prep(proj, g_q, g_k, rope_tabs, emit_k_f32=emit_cache)
    qn, kn, vnt = prep[:3]
    ya = _diff_attention(qn, kn, vnt, proj, g_sub, lam_params, cache_k4, cache_v4, layer_idx, lam_init)
    yf = _fourier_branch(proj)
    y2d = _out_projection(yf.reshape(b * n, F_DIM), ya.reshape(b * n, V_W), proj2d, x2d, mod_l,
                          w_fproj, w_aproj, w_out, rows_per_batch=n, mod_row0=mod_row0,
                          per_batch_mod=per_batch_mod)
    y = y2d.reshape(b, n, d)
    if emit_cache:
        k_new = prep[3].reshape(b, n, N_HEADS, 2, QK_DIM)
        v_new = prep[4].reshape(b, n, N_HEADS, V_DIM)
        return y, k_new, v_new
    return y, None, None


def kernel(x_prompt, x_sample, c, cache_k, cache_v, c_ctx, w_in, w_fproj, w_aproj, w_out, w_mod, b_mod, g_norm,
           g_q, g_k, g_sub, lam_q1, lam_k1, lam_q2, lam_k2):
    depth = w_in.shape[0]
    d = x_prompt.shape[-1]
    dec_b, dec_n = x_sample.shape[0], x_sample.shape[1]
    assert 1 + dec_b <= MOD_ROWS

    cvecs = jnp.concatenate([c_ctx[None, :], c, jnp.zeros((MOD_ROWS - 1 - dec_b, d), F32)], axis=0)
    mod = _modulation(cvecs, w_mod, b_mod)

    w_in_b, w_fproj_b, w_aproj_b, w_out_b = (w.astype(BF16) for w in (w_in, w_fproj, w_aproj, w_out))
    lam_params = jnp.stack([lam_q1, lam_k1, lam_q2, lam_k2], axis=1)
    rope_tabs = _rope_tables(dec_n)
    past = cache_k.shape[2]
    cache_k4 = cache_k.reshape(dec_b, depth, past, QK_W)
    cache_v4 = cache_v.reshape(dec_b, depth, past, V_W)

    y_p, y_s = x_prompt, x_sample
    ks_out, vs_out = [], []
    for l in range(depth):
        lam_init = 0.8 - 0.6 * math.exp(-0.3 * l)
        weights_l = (w_in_b[l], w_fproj_b[l], w_aproj_b[l], w_out_b[l], g_norm[l], g_q[l], g_k[l], g_sub[l],
                     lam_params[l])
        y_p, k_l, v_l = _layer(y_p, mod[l], weights_l, lam_init, mod_row0=0, per_batch_mod=False, rope_tabs=None,
                               cache_k4=None, cache_v4=None, layer_idx=l, emit_cache=True)
        ks_out.append(k_l)
        vs_out.append(v_l)
        y_s, _, _ = _layer(y_s, mod[l], weights_l, lam_init, mod_row0=1, per_batch_mod=True, rope_tabs=rope_tabs,
                           cache_k4=cache_k4, cache_v4=cache_v4, layer_idx=l, emit_cache=False)
    return (y_p, y_s, jnp.stack(ks_out, axis=1), jnp.stack(vs_out, axis=1))
```
